```python
import functools
import jax, jax.numpy as jnp
from jax import lax
import numpy as np

D_MODEL = 2048
BATCH = 4
SEQ = 2048
DEPTH = 4
DEC_BATCH = 128
DEC_SEQ = 8
PAST_LEN = 16384
PAGE_SIZE = 128

MIX_DIM = D_MODEL
CONV_DIM = MIX_DIM // 2
POOL_DIM = MIX_DIM - CONV_DIM
CONV_W = 3
POOL_WINDOWS = (2, 4, 8, 16)
POOL_GROUPS = len(POOL_WINDOWS)
POOL_GROUP_DIM = POOL_DIM // POOL_GROUPS
POOL_BUF = max(POOL_WINDOWS) - 1
IN_DIM = 3 * CONV_DIM + POOL_DIM
FFN_DIM = ((8 * D_MODEL // 3 + 255) // 256) * 256
N_EXPERTS = 8
TOP_K = 2
EXPERT_DIM = FFN_DIM // 2
N_DENSE = (DEPTH + 1) // 2
N_MOE = DEPTH // 2
EPS = 1e-6

kernel_name = "hybrid_shortconv_pool_moe_decode_step"


def _rmsnorm(x, g):
    xf = x.astype(jnp.float32)
    xf = xf * lax.rsqrt(jnp.mean(xf * xf, axis=-1, keepdims=True) + EPS)
    return xf.astype(x.dtype) * g


def _mixer(xn, conv_buf, pool_buf, pos0, w_in, conv_w, pool_w, pool_scale, w_out):
    b, T, _ = xn.shape
    proj = jnp.einsum('btd,de->bte', xn, w_in)
    b_g, c_g, h, u = jnp.split(proj, [CONV_DIM, 2 * CONV_DIM, 3 * CONV_DIM], axis=-1)
    v = c_g * h
    vp = jnp.concatenate([conv_buf.astype(v.dtype), v], axis=1)
    conv = conv_w[0] * vp[:, 0:T]
    for k in range(1, CONV_W):
        conv = conv + conv_w[k] * vp[:, k:k + T]
    y_conv = b_g * conv
    up = jnp.concatenate([pool_buf.astype(u.dtype), u], axis=1)
    upf = up.astype(jnp.float32)
    cs = jnp.concatenate([jnp.zeros_like(upf[:, :1]), jnp.cumsum(upf, axis=1)], axis=1)
    uf = u.astype(jnp.float32)
    pos = pos0 + jnp.arange(T)
    groups = []
    for g, w in enumerate(POOL_WINDOWS):
        sl = slice(g * POOL_GROUP_DIM, (g + 1) * POOL_GROUP_DIM)
        s = cs[:, POOL_BUF + 1:POOL_BUF + 1 + T, sl] - cs[:, POOL_BUF + 1 - w:POOL_BUF + 1 - w + T, sl]
        cnt = jnp.minimum(pos + 1, w).astype(jnp.float32)[None, :, None]
        groups.append(s / cnt - uf[..., sl])
    pooled = jnp.stack(groups, axis=2).astype(xn.dtype)
    z = jnp.einsum('btgc,gce->btge', pooled, pool_w).reshape(b, T, POOL_DIM) * pool_scale
    out = jnp.einsum('btm,md->btd', jnp.concatenate([y_conv, z], axis=-1), w_out)
    return out, vp[:, -(CONV_W - 1):], up[:, -POOL_BUF:]


def _swiglu(x, w1, w3, w2):
    return jnp.einsum('btf,fd->btd', jax.nn.silu(jnp.einsum('btd,df->btf', x, w1)) * jnp.einsum('btd,df->btf', x, w3), w2)


def _moe(x, router_w, w1, w3, w2):
    logits = jnp.einsum('btd,de->bte', x, router_w).astype(jnp.float32)
    top_v, top_i = lax.top_k(logits, TOP_K)
    top_p = jax.nn.softmax(top_v, axis=-1)
    gates = jnp.sum(jax.nn.one_hot(top_i, N_EXPERTS, dtype=jnp.float32) * top_p[..., None], axis=-2).astype(x.dtype)
    out = jnp.zeros_like(x)
    for e in range(N_EXPERTS):
        out = out + gates[..., e:e + 1] * _swiglu(x, w1[e], w3[e], w2[e])
    return out


def _layer(x, conv_buf, pool_buf, pos0, g_mix, g_ffn, mix_w, ffn):
    m, new_conv, new_pool = _mixer(_rmsnorm(x, g_mix), conv_buf, pool_buf, pos0, *mix_w)
    x = x + m
    x = x + ffn(_rmsnorm(x, g_ffn))
    return x, new_conv, new_pool


def setup_inputs(seed: int = 0) -> dict:
    key = jax.random.key(seed)
    ks = jax.random.split(key, 20)
    f32 = jnp.float32
    nrm = lambda k, shape, scale: jax.random.normal(k, shape, f32) * scale
    return {
        "x_prompt": nrm(ks[0], (BATCH, SEQ, D_MODEL), 1.0),
        "x_sample": nrm(ks[1], (DEC_BATCH, DEC_SEQ, D_MODEL), 1.0),
        "state_conv": nrm(ks[2], (DEPTH, DEC_BATCH, CONV_W - 1, CONV_DIM), 1.0),
        "state_pool": nrm(ks[3], (DEPTH, DEC_BATCH, POOL_BUF, POOL_DIM), 1.0),
        "norm_mix": 1.0 + nrm(ks[4], (DEPTH, D_MODEL), 0.02),
        "norm_ffn": 1.0 + nrm(ks[5], (DEPTH, D_MODEL), 0.02),
        "w_in": nrm(ks[6], (DEPTH, D_MODEL, IN_DIM), D_MODEL ** -0.5),
        "conv_w": nrm(ks[7], (DEPTH, CONV_W, CONV_DIM), CONV_W ** -0.5),
        "pool_w": nrm(ks[8], (DEPTH, POOL_GROUPS, POOL_GROUP_DIM, POOL_GROUP_DIM), POOL_GROUP_DIM ** -0.5),
        "pool_scale": 1.0 + nrm(ks[9], (DEPTH, POOL_DIM), 0.02),
        "w_out": nrm(ks[10], (DEPTH, MIX_DIM, D_MODEL), MIX_DIM ** -0.5),
        "dense_w1": nrm(ks[11], (N_DENSE, D_MODEL, FFN_DIM), D_MODEL ** -0.5),
        "dense_w3": nrm(ks[12], (N_DENSE, D_MODEL, FFN_DIM), D_MODEL ** -0.5),
        "dense_w2": nrm(ks[13], (N_DENSE, FFN_DIM, D_MODEL), FFN_DIM ** -0.5),
        "router_w": nrm(ks[14], (N_MOE, D_MODEL, N_EXPERTS), D_MODEL ** -0.5),
        "moe_w1": nrm(ks[15], (N_MOE, N_EXPERTS, D_MODEL, EXPERT_DIM), D_MODEL ** -0.5),
        "moe_w3": nrm(ks[16], (N_MOE, N_EXPERTS, D_MODEL, EXPERT_DIM), D_MODEL ** -0.5),
        "moe_w2": nrm(ks[17], (N_MOE, N_EXPERTS, EXPERT_DIM, D_MODEL), EXPERT_DIM ** -0.5),
        "final_norm": 1.0 + nrm(ks[18], (D_MODEL,), 0.02),
    }


def reference(x_prompt, x_sample, state_conv, state_pool, norm_mix, norm_ffn, w_in, conv_w, pool_w,
              pool_scale, w_out, dense_w1, dense_w3, dense_w2, router_w, moe_w1, moe_w3, moe_w2, final_norm):
    b_p = x_prompt.shape[0]
    zero_conv = jnp.zeros((b_p, CONV_W - 1, CONV_DIM), x_prompt.dtype)
    zero_pool = jnp.zeros((b_p, POOL_BUF, POOL_DIM), x_prompt.dtype)
    hp, hs = x_prompt, x_sample
    conv_p, pool_p, conv_s, pool_s = [], [], [], []
    for l in range(DEPTH):
        mix_w = (w_in[l], conv_w[l], pool_w[l], pool_scale[l], w_out[l])
        i = l // 2
        if l % 2 == 0:
            ffn = functools.partial(_swiglu, w1=dense_w1[i], w3=dense_w3[i], w2=dense_w2[i])
        else:
            ffn = functools.partial(_moe, router_w=router_w[i], w1=moe_w1[i], w3=moe_w3[i], w2=moe_w2[i])
        hp, cp, pp = _layer(hp, zero_conv, zero_pool, 0, norm_mix[l], norm_ffn[l], mix_w, ffn)
        hs, cs_, ps_ = _layer(hs, state_conv[l], state_pool[l], PAST_LEN, norm_mix[l], norm_ffn[l], mix_w, ffn)
        conv_p.append(cp); pool_p.append(pp); conv_s.append(cs_); pool_s.append(ps_)
    y_prompt = _rmsnorm(hp, final_norm)
    y_sample = _rmsnorm(hs, final_norm)
    new_conv_prompt = jnp.stack(conv_p, axis=0)
    new_pool_prompt = jnp.stack(pool_p, axis=0)
    new_conv_sample = jnp.stack(conv_s, axis=0)
    new_pool_sample = jnp.stack(pool_s, axis=0)
    return (y_prompt, y_sample, new_conv_prompt, new_pool_prompt, new_conv_sample, new_pool_sample)
```

```python
import functools

import jax
import jax.numpy as jnp
from jax import lax
from jax.experimental import pallas as pl
from jax.experimental.pallas import tpu as pltpu

f32 = jnp.float32
bf16 = jnp.bfloat16

D_MODEL = 2048
CONV_DIM = 1024
POOL_DIM = 1024
IN_DIM = 3 * CONV_DIM + POOL_DIM
CONV_W = 3
POOL_WINDOWS = (2, 4, 8, 16)
POOL_GROUP_DIM = POOL_DIM // len(POOL_WINDOWS)
POOL_BUF = max(POOL_WINDOWS) - 1
N_EXPERTS = 8
PAST_LEN = 16384
EPS = 1e-6

LANES = 128
HIST = 16
CONV_HIST = 8
MIX_ROWS = 256
VMEM_LIMIT = 56 * 1024 * 1024


def _params(*sem):
    return pltpu.CompilerParams(dimension_semantics=sem, vmem_limit_bytes=VMEM_LIMIT)


def _norm_kernel(x_ref, g_ref, *rest, with_router, out_dtype):
    if with_router:
        rw_ref, xn_ref, gates_ref = rest
    else:
        (xn_ref,) = rest
    x = x_ref[...]
    xf = x * lax.rsqrt(jnp.mean(x * x, axis=-1, keepdims=True) + EPS)
    xn = xf * g_ref[...]
    xn_ref[...] = xn.astype(out_dtype)
    if with_router:
        logits = jnp.dot(xn, rw_ref[...], precision=lax.Precision.HIGHEST, preferred_element_type=f32)
        lane = lax.broadcasted_iota(jnp.int32, logits.shape, 1)
        neg = jnp.float32(-jnp.inf)
        logits = jnp.where(lane < N_EXPERTS, logits, neg)
        m1 = jnp.max(logits, axis=-1, keepdims=True)
        i1 = jnp.min(jnp.where(logits == m1, lane, LANES), axis=-1, keepdims=True)
        rest_l = jnp.where(lane == i1, neg, logits)
        m2 = jnp.max(rest_l, axis=-1, keepdims=True)
        i2 = jnp.min(jnp.where(rest_l == m2, lane, LANES), axis=-1, keepdims=True)
        e = jnp.exp(m2 - m1)
        denom = 1.0 + e
        gates_ref[...] = jnp.where(lane == i1, 1.0 / denom, 0.0) + jnp.where(lane == i2, e / denom, 0.0)


def _rmsnorm(x, g, *, router_w=None, out_dtype=bf16, row_offset=0, rows=None, tm=512):
    t_all, d = x.shape
    rows = t_all if rows is None else rows
    off = row_offset // tm
    with_router = router_w is not None
    in_specs = [pl.BlockSpec((tm, d), lambda i: (i + off, 0)), pl.BlockSpec((1, d), lambda i: (0, 0))]
    args = [x, g.reshape(1, d)]
    out_shape = [jax.ShapeDtypeStruct((rows, d), out_dtype)]
    out_specs = [pl.BlockSpec((tm, d), lambda i: (i, 0))]
    if with_router:
        rw = jnp.pad(router_w, ((0, 0), (0, LANES - router_w.shape[1])))
        in_specs.append(pl.BlockSpec((d, LANES), lambda i: (0, 0)))
        args.append(rw)
        out_shape.append(jax.ShapeDtypeStruct((rows, LANES), f32))
        out_specs.append(pl.BlockSpec((tm, LANES), lambda i: (i, 0)))
    res = pl.pallas_call(
        functools.partial(_norm_kernel, with_router=with_router, out_dtype=out_dtype),
        grid=(rows // tm,), in_specs=in_specs, out_specs=out_specs, out_shape=out_shape,
        compiler_params=_params("arbitrary"), name="rmsnorm_router" if with_router else "rmsnorm",
    )(*args)
    return res if with_router else res[0]


def _mm_kernel(a_ref, w_ref, *rest, has_res):
    if has_res:
        res_ref, o_ref, wb_ref = rest
    else:
        o_ref, wb_ref = rest

    @pl.when(pl.program_id(1) == 0)
    def _():
        wb_ref[...] = w_ref[...].astype(bf16)

    acc = jnp.dot(a_ref[...], wb_ref[...], preferred_element_type=f32)
    if has_res:
        acc = res_ref[...] + acc
    o_ref[...] = acc.astype(o_ref.dtype)


def _matmul(a, w, *, res=None, out_dtype=f32, tm, tn):
    t, k = a.shape
    n = w.shape[1]
    in_specs = [pl.BlockSpec((tm, k), lambda j, i: (i, 0)), pl.BlockSpec((k, tn), lambda j, i: (0, j))]
    args = [a, w]
    if res is not None:
        in_specs.append(pl.BlockSpec((tm, tn), lambda j, i: (i, j)))
        args.append(res)
    return pl.pallas_call(
        functools.partial(_mm_kernel, has_res=res is not None),
        grid=(n // tn, t // tm), in_specs=in_specs,
        out_specs=pl.BlockSpec((tm, tn), lambda j, i: (i, j)),
        out_shape=jax.ShapeDtypeStruct((t, n), out_dtype),
        scratch_shapes=[pltpu.VMEM((k, tn), bf16)],
        compiler_params=_params("arbitrary", "arbitrary"), name="matmul",
    )(*args)


def _swiglu_kernel(a_ref, w1_ref, w3_ref, o_ref, w1b_ref, w3b_ref):
    @pl.when(pl.program_id(1) == 0)
    def _():
        w1b_ref[...] = w1_ref[...].astype(bf16)
        w3b_ref[...] = w3_ref[...].astype(bf16)

    a = a_ref[...]
    h1 = jnp.dot(a, w1b_ref[...], preferred_element_type=f32)
    h3 = jnp.dot(a, w3b_ref[...], preferred_element_type=f32)
    o_ref[...] = (h1 * jax.nn.sigmoid(h1) * h3).astype(o_ref.dtype)


def _swiglu_hidden(a, w1, w3, *, tm, tf):
    t, k = a.shape
    if w1.ndim == 2:
        f_total = w1.shape[1]
        w_spec = pl.BlockSpec((k, tf), lambda j, i: (0, j))
    else:
        nf = w1.shape[2] // tf
        f_total = w1.shape[0] * w1.shape[2]
        w_spec = pl.BlockSpec((None, k, tf), lambda j, i: (j // nf, 0, j % nf))
    return pl.pallas_call(
        _swiglu_kernel,
        grid=(f_total // tf, t // tm),
        in_specs=[pl.BlockSpec((tm, k), lambda j, i: (i, 0)), w_spec, w_spec],
        out_specs=pl.BlockSpec((tm, tf), lambda j, i: (i, j)),
        out_shape=jax.ShapeDtypeStruct((t, f_total), bf16),
        scratch_shapes=[pltpu.VMEM((k, tf), bf16), pltpu.VMEM((k, tf), bf16)],
        compiler_params=_params("arbitrary", "arbitrary"), name="swiglu_hidden",
    )(a, w1, w3)


def _moe_down_kernel(a_ref, w_ref, gates_ref, res_ref, o_ref, acc_ref):
    e = pl.program_id(2)

    @pl.when(e == 0)
    def _():
        acc_ref[...] = jnp.zeros_like(acc_ref)

    y = jnp.dot(a_ref[...], w_ref[...].astype(bf16), preferred_element_type=f32)
    gates = gates_ref[...]
    lane = lax.broadcasted_iota(jnp.int32, gates.shape, 1)
    gate = jnp.sum(jnp.where(lane == e, gates, 0.0), axis=-1, keepdims=True)
    acc_ref[...] += gate * y

    @pl.when(e == pl.num_programs(2) - 1)
    def _():
        o_ref[...] = res_ref[...] + acc_ref[...]


def _moe_down(hidden, w2, gates, res, *, tm, tn):
    t = hidden.shape[0]
    n_e, k, n = w2.shape
    return pl.pallas_call(
        _moe_down_kernel,
        grid=(n // tn, t // tm, n_e),
        in_specs=[pl.BlockSpec((tm, k), lambda j, i, e: (i, e)),
                  pl.BlockSpec((None, k, tn), lambda j, i, e: (e, 0, j)),
                  pl.BlockSpec((tm, LANES), lambda j, i, e: (i, 0)),
                  pl.BlockSpec((tm, tn), lambda j, i, e: (i, j))],
        out_specs=pl.BlockSpec((tm, tn), lambda j, i, e: (i, j)),
        out_shape=jax.ShapeDtypeStruct((t, n), f32),
        scratch_shapes=[pltpu.VMEM((tm, tn), f32)],
        compiler_params=_params("arbitrary", "arbitrary", "arbitrary"), name="moe_down",
    )(hidden, w2, gates, res)


def _mixer_kernel(proj_ref, hv_ref, hu_ref, cw_ref, pw_ref, ps_ref,
                  ycat_ref, ncp_ref, npp_ref, ncs_ref, nps_ref, ev_ref, eu_ref,
                  *, prompt_steps, steps_per_seq, seqs_per_step):
    i = pl.program_id(0)
    rows = MIX_ROWS
    b_g = proj_ref[:, 0:CONV_DIM]
    v = proj_ref[:, CONV_DIM:2 * CONV_DIM] * proj_ref[:, 2 * CONV_DIM:3 * CONV_DIM]
    u = proj_ref[:, 3 * CONV_DIM:]

    def conv_and_sums(n):
        conv = cw_ref[0:1, :] * ev_ref[HIST - 2:HIST - 2 + n, :]
        conv = conv + cw_ref[1:2, :] * ev_ref[HIST - 1:HIST - 1 + n, :]
        conv = conv + cw_ref[2:3, :] * ev_ref[HIST:HIST + n, :]
        sums = []
        for g, w in enumerate(POOL_WINDOWS):
            lo, hi = g * POOL_GROUP_DIM, (g + 1) * POOL_GROUP_DIM
            s = eu_ref[HIST:HIST + n, lo:hi]
            for j in range(1, w):
                s = s + eu_ref[HIST - j:HIST - j + n, lo:hi]
            sums.append(s)
        return conv, sums

    def finish(conv, sums, cnts):
        y_conv = b_g * conv
        zs = []
        for g in range(len(POOL_WINDOWS)):
            lo, hi = g * POOL_GROUP_DIM, (g + 1) * POOL_GROUP_DIM
            pooled = sums[g] / cnts[g] - u[:, lo:hi]
            zs.append(jnp.dot(pooled.astype(bf16), pw_ref[g].astype(bf16), preferred_element_type=f32))
        z = jnp.concatenate(zs, axis=-1) * ps_ref[...]
        ycat_ref[:, 0:CONV_DIM] = y_conv.astype(ycat_ref.dtype)
        ycat_ref[:, CONV_DIM:] = z.astype(ycat_ref.dtype)

    @pl.when(i < prompt_steps)
    def _prompt():
        t = i % steps_per_seq

        @pl.when(t == 0)
        def _():
            ev_ref[0:HIST, :] = jnp.zeros((HIST, CONV_DIM), f32)
            eu_ref[0:HIST, :] = jnp.zeros((HIST, POOL_DIM), f32)

        @pl.when(t > 0)
        def _():
            ev_ref[0:HIST, :] = ev_ref[rows:rows + HIST, :]
            eu_ref[0:HIST, :] = eu_ref[rows:rows + HIST, :]

        ev_ref[HIST:HIST + rows, :] = v
        eu_ref[HIST:HIST + rows, :] = u
        conv, sums = conv_and_sums(rows)
        pos1 = t * rows + lax.broadcasted_iota(jnp.int32, (rows, 1), 0) + 1
        cnts = [jnp.minimum(pos1, w).astype(f32) for w in POOL_WINDOWS]
        finish(conv, sums, cnts)

        @pl.when(t == steps_per_seq - 1)
        def _():
            ncp_ref[0] = ev_ref[HIST + rows - (CONV_W - 1):HIST + rows, :]
            npp_ref[0] = eu_ref[HIST + rows - POOL_BUF:HIST + rows, :]

    @pl.when(i >= prompt_steps)
    def _sample():
        s = seqs_per_step
        new = rows // s
        v3 = v.reshape(s, new, CONV_DIM)
        u3 = u.reshape(s, new, POOL_DIM)
        extv = jnp.concatenate([hv_ref[...], v3], axis=1)
        extu = jnp.concatenate([hu_ref[...], u3], axis=1)
        nv, nu = s * (CONV_HIST + new), s * (HIST + new)
        ev_ref[HIST:HIST + nv, :] = extv.reshape(nv, CONV_DIM)
        eu_ref[HIST:HIST + nu, :] = extu.reshape(nu, POOL_DIM)
        conv = cw_ref[0:1, :] * ev_ref[HIST - 2:HIST - 2 + nv, :]
        conv = conv + cw_ref[1:2, :] * ev_ref[HIST - 1:HIST - 1 + nv, :]
        conv = conv + cw_ref[2:3, :] * ev_ref[HIST:HIST + nv, :]
        conv = conv.reshape(s, CONV_HIST + new, CONV_DIM)[:, CONV_HIST:, :].reshape(rows, CONV_DIM)
        sums = []
        for g, w in enumerate(POOL_WINDOWS):
            lo, hi = g * POOL_GROUP_DIM, (g + 1) * POOL_GROUP_DIM
            sg = eu_ref[HIST:HIST + nu, lo:hi]
            for j in range(1, w):
                sg = sg + eu_ref[HIST - j:HIST - j + nu, lo:hi]
            sums.append(sg.reshape(s, HIST + new, POOL_GROUP_DIM)[:, HIST:, :].reshape(rows, POOL_GROUP_DIM))
        cnts = [jnp.float32(min(PAST_LEN + 1, w)) for w in POOL_WINDOWS]
        finish(conv, sums, cnts)
        ncs_ref[...] = v3[:, new - (CONV_W - 1):, :]
        nps_ref[...] = extu[:, HIST + new - POOL_BUF:, :]


def _mixer(proj, hv, hu, conv_w, pool_w, pool_scale, *, n_prompt_seq, prompt_len, n_sample_seq, sample_len):
    t_all = proj.shape[0]
    rows = MIX_ROWS
    steps_per_seq = prompt_len // rows
    prompt_steps = n_prompt_seq * steps_per_seq
    seqs_per_step = rows // sample_len
    sample_steps = n_sample_seq // seqs_per_step
    last_p = n_prompt_seq - 1

    def samp(i):
        return jnp.maximum(i - prompt_steps, 0)

    def pseq(i):
        return jnp.minimum(i // steps_per_seq, last_p)

    ext_rows = HIST + max(rows + HIST, seqs_per_step * (HIST + sample_len))
    kern = functools.partial(_mixer_kernel, prompt_steps=prompt_steps, steps_per_seq=steps_per_seq,
                             seqs_per_step=seqs_per_step)
    return pl.pallas_call(
        kern,
        grid=(prompt_steps + sample_steps,),
        in_specs=[pl.BlockSpec((rows, IN_DIM), lambda i: (i, 0)),
                  pl.BlockSpec((seqs_per_step, CONV_HIST, CONV_DIM), lambda i: (samp(i), 0, 0)),
                  pl.BlockSpec((seqs_per_step, HIST, POOL_DIM), lambda i: (samp(i), 0, 0)),
                  pl.BlockSpec((CONV_W, CONV_DIM), lambda i: (0, 0)),
                  pl.BlockSpec(pool_w.shape, lambda i: (0, 0, 0)),
                  pl.BlockSpec((1, POOL_DIM), lambda i: (0, 0))],
        out_specs=[pl.BlockSpec((rows, CONV_DIM + POOL_DIM), lambda i: (i, 0)),
                   pl.BlockSpec((1, CONV_W - 1, CONV_DIM), lambda i: (pseq(i), 0, 0)),
                   pl.BlockSpec((1, POOL_BUF, POOL_DIM), lambda i: (pseq(i), 0, 0)),
                   pl.BlockSpec((seqs_per_step, CONV_W - 1, CONV_DIM), lambda i: (samp(i), 0, 0)),
                   pl.BlockSpec((seqs_per_step, POOL_BUF, POOL_DIM), lambda i: (samp(i), 0, 0))],
        out_shape=[jax.ShapeDtypeStruct((t_all, CONV_DIM + POOL_DIM), bf16),
                   jax.ShapeDtypeStruct((n_prompt_seq, CONV_W - 1, CONV_DIM), f32),
                   jax.ShapeDtypeStruct((n_prompt_seq, POOL_BUF, POOL_DIM), f32),
                   jax.ShapeDtypeStruct((n_sample_seq, CONV_W - 1, CONV_DIM), f32),
                   jax.ShapeDtypeStruct((n_sample_seq, POOL_BUF, POOL_DIM), f32)],
        scratch_shapes=[pltpu.VMEM((ext_rows, CONV_DIM), f32), pltpu.VMEM((ext_rows, POOL_DIM), f32)],
        compiler_params=_params("arbitrary"), name="mixer",
    )(proj, hv, hu, conv_w, pool_w, pool_scale.reshape(1, POOL_DIM))


def kernel(x_prompt, x_sample, state_conv, state_pool, norm_mix, norm_ffn, w_in, conv_w, pool_w, pool_scale,
           w_out, dense_w1, dense_w3, dense_w2, router_w, moe_w1, moe_w3, moe_w2, final_norm):
    n_p, len_p, d = x_prompt.shape
    n_s, len_s, _ = x_sample.shape
    t_p, t_s = n_p * len_p, n_s * len_s
    depth = w_in.shape[0]
    x = jnp.concatenate([x_prompt.reshape(t_p, d), x_sample.reshape(t_s, d)], axis=0)
    hv_all = jnp.pad(state_conv, ((0, 0), (0, 0), (CONV_HIST - (CONV_W - 1), 0), (0, 0)))
    hu_all = jnp.pad(state_pool, ((0, 0), (0, 0), (HIST - POOL_BUF, 0), (0, 0)))

    conv_p, pool_p, conv_s, pool_s = [], [], [], []
    for l in range(depth):
        xn = _rmsnorm(x, norm_mix[l])
        proj = _matmul(xn, w_in[l], tm=1024, tn=512)
        ycat, ncp, npp, ncs, nps = _mixer(proj, hv_all[l], hu_all[l], conv_w[l], pool_w[l], pool_scale[l],
                                          n_prompt_seq=n_p, prompt_len=len_p, n_sample_seq=n_s, sample_len=len_s)
        conv_p.append(ncp); pool_p.append(npp); conv_s.append(ncs); pool_s.append(nps)
        h = _matmul(ycat, w_out[l], res=x, tm=1024, tn=512)
        i = l // 2
        if l % 2 == 0:
            hn = _rmsnorm(h, norm_ffn[l])
            hid = _swiglu_hidden(hn, dense_w1[i], dense_w3[i], tm=1024, tf=512)
            x = _matmul(hid, dense_w2[i], res=h, tm=1024, tn=256)
        else:
            hn, gates = _rmsnorm(h, norm_ffn[l], router_w=router_w[i])
            hid = _swiglu_hidden(hn, moe_w1[i], moe_w3[i], tm=1024, tf=256)
            x = _moe_down(hid, moe_w2[i], gates, h, tm=1024, tn=512)

    y_prompt = _rmsnorm(x, final_norm, out_dtype=f32, rows=t_p).reshape(n_p, len_p, d)
    y_sample = _rmsnorm(x, final_norm, out_dtype=f32, row_offset=t_p, rows=t_s).reshape(n_s, len_s, d)
    return (y_prompt, y_sample, jnp.stack(conv_p), jnp.stack(pool_p), jnp.stack(conv_s), jnp.stack(pool_s))
```

```python
import functools

import jax
import jax.numpy as jnp
from jax import lax
from jax.experimental import pallas as pl
from jax.experimental.pallas import tpu as pltpu

f32 = jnp.float32
bf16 = jnp.bfloat16
i32 = jnp.int32

D_MODEL = 2048
CONV_DIM = 1024
POOL_DIM = 1024
IN_DIM = 3 * CONV_DIM + POOL_DIM
CONV_W = 3
POOL_WINDOWS = (2, 4, 8, 16)
POOL_GROUP_DIM = POOL_DIM // len(POOL_WINDOWS)
POOL_BUF = max(POOL_WINDOWS) - 1
N_EXPERTS = 8
TOP_K = 2
PAST_LEN = 16384
EPS = 1e-6

LANES = 128
HIST = 16
CONV_HIST = 8
MIX_ROWS = 256
MLP_ROWS = 1024
MLP_SUB = 256
MLP_NSUB = MLP_ROWS // MLP_SUB
MLP_FC = 256
COMBINE_ROWS = 256
DMA_UNROLL = 8
VMEM_LIMIT = 56 * 1024 * 1024


def _params(*sem):
    return pltpu.CompilerParams(dimension_semantics=sem, vmem_limit_bytes=VMEM_LIMIT)


def _norm_kernel(x_ref, g_ref, *rest, with_router):
    if with_router:
        rw_ref, xn_ref, gates_ref, idx_ref = rest
    else:
        (xn_ref,) = rest
    x = x_ref[...]
    xf = x * lax.rsqrt(jnp.mean(x * x, axis=-1, keepdims=True) + EPS)
    xn = xf * g_ref[...]
    xn_ref[...] = xn.astype(xn_ref.dtype)
    if with_router:
        logits = jnp.dot(xn, rw_ref[...], precision=lax.Precision.HIGHEST, preferred_element_type=f32)
        lane = lax.broadcasted_iota(i32, logits.shape, 1)
        neg = jnp.float32(-jnp.inf)
        logits = jnp.where(lane < N_EXPERTS, logits, neg)
        m1 = jnp.max(logits, axis=-1, keepdims=True)
        i1 = jnp.min(jnp.where(logits == m1, lane, LANES), axis=-1, keepdims=True)
        rest_l = jnp.where(lane == i1, neg, logits)
        m2 = jnp.max(rest_l, axis=-1, keepdims=True)
        i2 = jnp.min(jnp.where(rest_l == m2, lane, LANES), axis=-1, keepdims=True)
        e = jnp.exp(m2 - m1)
        denom = 1.0 + e
        gates_ref[...] = jnp.where(lane == 0, 1.0 / denom, jnp.where(lane == 1, e / denom, 0.0))
        idx_ref[...] = jnp.where(lane == 0, i1, jnp.where(lane == 1, i2, 0))


def _rmsnorm(x, g, *, router_w=None, out_dtype=bf16, row_offset=0, rows=None, tm=512):
    t_all, d = x.shape
    rows = t_all if rows is None else rows
    off = row_offset // tm
    with_router = router_w is not None
    in_specs = [pl.BlockSpec((tm, d), lambda i: (i + off, 0)), pl.BlockSpec((1, d), lambda i: (0, 0))]
    args = [x, g.reshape(1, d)]
    out_shape = [jax.ShapeDtypeStruct((rows, d), out_dtype)]
    out_specs = [pl.BlockSpec((tm, d), lambda i: (i, 0))]
    if with_router:
        rw = jnp.pad(router_w, ((0, 0), (0, LANES - router_w.shape[1])))
        in_specs.append(pl.BlockSpec((d, LANES), lambda i: (0, 0)))
        args.append(rw)
        out_shape += [jax.ShapeDtypeStruct((rows, LANES), f32), jax.ShapeDtypeStruct((rows, LANES), i32)]
        out_specs += [pl.BlockSpec((tm, LANES), lambda i: (i, 0)), pl.BlockSpec((tm, LANES), lambda i: (i, 0))]
    res = pl.pallas_call(
        functools.partial(_norm_kernel, with_router=with_router),
        grid=(rows // tm,), in_specs=in_specs, out_specs=out_specs, out_shape=out_shape,
        compiler_params=_params("arbitrary"), name="rmsnorm_router" if with_router else "rmsnorm",
    )(*args)
    return res if with_router else res[0]


def _mm_kernel(a_ref, w_ref, *rest, has_res):
    if has_res:
        res_ref, o_ref, wb_ref = rest
    else:
        o_ref, wb_ref = rest

    @pl.when(pl.program_id(1) == 0)
    def _():
        wb_ref[...] = w_ref[...].astype(bf16)

    acc = jnp.dot(a_ref[...], wb_ref[...], preferred_element_type=f32)
    if has_res:
        acc = res_ref[...] + acc
    o_ref[...] = acc.astype(o_ref.dtype)


def _matmul(a, w, layer, *, res=None, out_dtype=f32, tm, tn):
    t, k = a.shape
    n = w.shape[2]
    in_specs = [pl.BlockSpec((tm, k), lambda j, i: (i, 0)), pl.BlockSpec((None, k, tn), lambda j, i: (layer, 0, j))]
    args = [a, w]
    if res is not None:
        in_specs.append(pl.BlockSpec((tm, tn), lambda j, i: (i, j)))
        args.append(res)
    return pl.pallas_call(
        functools.partial(_mm_kernel, has_res=res is not None),
        grid=(n // tn, t // tm), in_specs=in_specs,
        out_specs=pl.BlockSpec((tm, tn), lambda j, i: (i, j)),
        out_shape=jax.ShapeDtypeStruct((t, n), out_dtype),
        scratch_shapes=[pltpu.VMEM((k, tn), bf16)],
        compiler_params=_params("arbitrary", "arbitrary"), name="matmul",
    )(*args)


def _mlp_kernel(te_ref, ns_ref, tb_ref, x_ref, w1_ref, w3_ref, w2_ref, *rest, grouped):
    del te_ref, tb_ref
    if grouped:
        o_ref, xb_ref, w1b_ref, w3b_ref, w2b_ref = rest
    else:
        res_ref, o_ref, w1b_ref, w3b_ref, w2b_ref = rest
    g, c = pl.program_id(0), pl.program_id(1)
    nsub = ns_ref[g]
    xsrc = xb_ref if grouped else x_ref

    def cast_weights():
        w1b_ref[...] = w1_ref[...].astype(bf16)
        w3b_ref[...] = w3_ref[...].astype(bf16)
        w2b_ref[...] = w2_ref[...].astype(bf16)

    def chunk(x):
        h1 = jnp.dot(x, w1b_ref[...], preferred_element_type=f32)
        h3 = jnp.dot(x, w3b_ref[...], preferred_element_type=f32)
        hid = (h1 * jax.nn.sigmoid(h1) * h3).astype(bf16)
        return jnp.dot(hid, w2b_ref[...], preferred_element_type=f32)

    @pl.when(c == 0)
    def _init():
        if grouped:
            o_ref[...] = jnp.zeros_like(o_ref)

            @pl.when(nsub > 0)
            def _():
                xb_ref[...] = x_ref[...].astype(bf16)
        else:
            o_ref[...] = res_ref[...]

    @pl.when(nsub == MLP_NSUB)
    def _full():
        cast_weights()
        o_ref[...] += chunk(xsrc[...])

    if grouped:
        @pl.when((nsub > 0) & (nsub < MLP_NSUB))
        def _partial():
            cast_weights()

            def body(k, carry):
                r = pl.multiple_of(k * MLP_SUB, MLP_SUB)
                o_ref[pl.ds(r, MLP_SUB), :] += chunk(xsrc[pl.ds(r, MLP_SUB), :])
                return carry

            lax.fori_loop(0, nsub, body, 0)


def _mlp(x, w1, w3, w2, tile_expert, tile_nsub, tile_blk, *, res=None):
    grouped = res is None
    k = x.shape[1]
    f, n = w2.shape[1], w2.shape[2]
    n_tiles = tile_expert.shape[0]
    n_chunks = f // MLP_FC

    def wcol(g, c, te, ns, tb):
        return jnp.where(ns[g] > 0, c, n_chunks - 1)

    in_specs = [
        pl.BlockSpec((MLP_ROWS, k), lambda g, c, te, ns, tb: (tb[g], 0)),
        pl.BlockSpec((None, k, MLP_FC), lambda g, c, te, ns, tb: (te[g], 0, wcol(g, c, te, ns, tb))),
        pl.BlockSpec((None, k, MLP_FC), lambda g, c, te, ns, tb: (te[g], 0, wcol(g, c, te, ns, tb))),
        pl.BlockSpec((None, MLP_FC, n), lambda g, c, te, ns, tb: (te[g], wcol(g, c, te, ns, tb), 0)),
    ]
    args = [x, w1, w3, w2]
    scratch = [pltpu.VMEM((k, MLP_FC), bf16), pltpu.VMEM((k, MLP_FC), bf16), pltpu.VMEM((MLP_FC, n), bf16)]
    if grouped:
        scratch = [pltpu.VMEM((MLP_ROWS, k), bf16)] + scratch
    else:
        in_specs.append(pl.BlockSpec((MLP_ROWS, n), lambda g, c, te, ns, tb: (tb[g], 0),
                                     pipeline_mode=pl.Buffered(1)))
        args.append(res)
    return pl.pallas_call(
        functools.partial(_mlp_kernel, grouped=grouped),
        grid_spec=pltpu.PrefetchScalarGridSpec(
            num_scalar_prefetch=3, grid=(n_tiles, n_chunks), in_specs=in_specs,
            out_specs=pl.BlockSpec((MLP_ROWS, n), lambda g, c, te, ns, tb: (g, 0),
                                   pipeline_mode=pl.Buffered(1)),
            scratch_shapes=scratch),
        out_shape=jax.ShapeDtypeStruct((x.shape[0], n), f32),
        compiler_params=_params("arbitrary", "arbitrary"), name="mlp_grouped" if grouped else "mlp_dense",
    )(tile_expert, tile_nsub, tile_blk, *args)


def _route_layout(idx, n_tiles):
    t = idx.shape[0]
    flat_e = idx[:, :TOP_K].reshape(-1)
    onehot = (flat_e[:, None] == jnp.arange(N_EXPERTS, dtype=i32)[None, :]).astype(i32)
    csum = jnp.cumsum(onehot, axis=0)
    rank = jnp.take_along_axis(csum, flat_e[:, None], axis=1)[:, 0] - 1
    cnt = csum[-1]
    tiles_e = (cnt + MLP_ROWS - 1) // MLP_ROWS
    tile_end = jnp.cumsum(tiles_e)
    tile_start = tile_end - tiles_e
    pos = tile_start[flat_e] * MLP_ROWS + rank
    n_used = tile_end[-1]
    g = jnp.arange(n_tiles, dtype=i32)
    g_eff = jnp.minimum(g, n_used - 1)
    tile_e = jnp.minimum(jnp.searchsorted(tile_end, g_eff, side="right").astype(i32), N_EXPERTS - 1)
    rows_in = jnp.clip(cnt[tile_e] - (g_eff - tile_start[tile_e]) * MLP_ROWS, 0, MLP_ROWS)
    nsub = jnp.where(g < n_used, (rows_in + MLP_SUB - 1) // MLP_SUB, 0).astype(i32)
    src = jnp.zeros((n_tiles * MLP_ROWS,), i32).at[pos].set(jnp.arange(TOP_K * t, dtype=i32) // TOP_K)
    return pos.reshape(t, TOP_K), src, tile_e, nsub, g_eff.astype(i32)


def _row_copy(src_hbm, dst_ref, src_row, dst_row, sem):
    return pltpu.make_async_copy(src_hbm.at[pl.ds(src_row, 1)], dst_ref.at[pl.ds(dst_row, 1)], sem)


def _gather_kernel(ns_ref, src_ref, x_hbm, o_ref, sem):
    g = pl.program_id(0)
    nsub = ns_ref[g]
    groups = nsub * (MLP_SUB // DMA_UNROLL)

    def issue(q, carry):
        for u in range(DMA_UNROLL):
            r = q * DMA_UNROLL + u
            _row_copy(x_hbm, o_ref, src_ref[0, 0, r], r, sem).start()
        return carry

    def drain(q, carry):
        for u in range(DMA_UNROLL):
            _row_copy(x_hbm, o_ref, 0, 0, sem).wait()
        return carry

    lax.fori_loop(0, groups, issue, 0)
    lax.fori_loop(0, groups, drain, 0)
    for k in range(MLP_NSUB):
        @pl.when(nsub <= k)
        def _():
            o_ref[k * MLP_SUB:(k + 1) * MLP_SUB, :] = jnp.zeros((MLP_SUB, o_ref.shape[1]), o_ref.dtype)


def _gather_rows(x, src, tile_nsub):
    n_tiles = tile_nsub.shape[0]
    d = x.shape[1]
    return pl.pallas_call(
        _gather_kernel,
        grid_spec=pltpu.PrefetchScalarGridSpec(
            num_scalar_prefetch=1, grid=(n_tiles,),
            in_specs=[pl.BlockSpec((1, 1, MLP_ROWS), lambda g, ns: (g, 0, 0), memory_space=pltpu.SMEM),
                      pl.BlockSpec(memory_space=pl.ANY)],
            out_specs=pl.BlockSpec((MLP_ROWS, d), lambda g, ns: (g, 0)),
            scratch_shapes=[pltpu.SemaphoreType.DMA(())]),
        out_shape=jax.ShapeDtypeStruct((n_tiles * MLP_ROWS, d), x.dtype),
        compiler_params=_params("arbitrary"), name="moe_gather",
    )(tile_nsub, src.reshape(n_tiles, 1, MLP_ROWS), x)


def _combine_kernel(pos_ref, h_ref, gates_ref, y_hbm, o_ref, ya_ref, yb_ref, sem):
    def issue(q, carry):
        for u in range(DMA_UNROLL):
            r = q * DMA_UNROLL + u
            _row_copy(y_hbm, ya_ref, pos_ref[0, 0, TOP_K * r], r, sem.at[0]).start()
            _row_copy(y_hbm, yb_ref, pos_ref[0, 0, TOP_K * r + 1], r, sem.at[1]).start()
        return carry

    def drain(q, carry):
        for u in range(DMA_UNROLL):
            _row_copy(y_hbm, ya_ref, 0, 0, sem.at[0]).wait()
            _row_copy(y_hbm, yb_ref, 0, 0, sem.at[1]).wait()
        return carry

    lax.fori_loop(0, COMBINE_ROWS // DMA_UNROLL, issue, 0)
    lax.fori_loop(0, COMBINE_ROWS // DMA_UNROLL, drain, 0)
    gates = gates_ref[...]
    o_ref[...] = h_ref[...] + (gates[:, 0:1] * ya_ref[...] + gates[:, 1:2] * yb_ref[...])


def _combine(h, gates, pos, y):
    t, d = h.shape
    rows = COMBINE_ROWS
    return pl.pallas_call(
        _combine_kernel,
        grid=(t // rows,),
        in_specs=[pl.BlockSpec((1, 1, TOP_K * rows), lambda i: (i, 0, 0), memory_space=pltpu.SMEM),
                  pl.BlockSpec((rows, d), lambda i: (i, 0)),
                  pl.BlockSpec((rows, LANES), lambda i: (i, 0)),
                  pl.BlockSpec(memory_space=pl.ANY)],
        out_specs=pl.BlockSpec((rows, d), lambda i: (i, 0)),
        out_shape=jax.ShapeDtypeStruct((t, d), f32),
        scratch_shapes=[pltpu.VMEM((rows, d), f32), pltpu.VMEM((rows, d), f32), pltpu.SemaphoreType.DMA((2,))],
        compiler_params=_params("arbitrary"), name="moe_combine",
    )(pos.reshape(t // rows, 1, TOP_K * rows), h, gates, y)


def _mixer_kernel(proj_ref, hv_ref, hu_ref, cw_ref, pw_ref, ps_ref,
                  ycat_ref, ncp_ref, npp_ref, ncs_ref, nps_ref, ev_ref, eu_ref,
                  *, prompt_steps, steps_per_seq, seqs_per_step):
    i = pl.program_id(0)
    rows = MIX_ROWS
    b_g = proj_ref[:, 0:CONV_DIM]
    v = proj_ref[:, CONV_DIM:2 * CONV_DIM] * proj_ref[:, 2 * CONV_DIM:3 * CONV_DIM]
    u = proj_ref[:, 3 * CONV_DIM:]

    def conv_taps(n):
        conv = cw_ref[0:1, :] * ev_ref[HIST - 2:HIST - 2 + n, :]
        conv = conv + cw_ref[1:2, :] * ev_ref[HIST - 1:HIST - 1 + n, :]
        return conv + cw_ref[2:3, :] * ev_ref[HIST:HIST + n, :]

    def window_sum(g, n):
        lo, hi = g * POOL_GROUP_DIM, (g + 1) * POOL_GROUP_DIM
        s = eu_ref[HIST:HIST + n, lo:hi]
        for j in range(1, POOL_WINDOWS[g]):
            s = s + eu_ref[HIST - j:HIST - j + n, lo:hi]
        return s

    def finish(conv, sums, cnts):
        y_conv = b_g * conv
        zs = []
        for g in range(len(POOL_WINDOWS)):
            lo, hi = g * POOL_GROUP_DIM, (g + 1) * POOL_GROUP_DIM
            pooled = sums[g] / cnts[g] - u[:, lo:hi]
            zs.append(jnp.dot(pooled.astype(bf16), pw_ref[g].astype(bf16), preferred_element_type=f32))
        z = jnp.concatenate(zs, axis=-1) * ps_ref[...]
        ycat_ref[:, 0:CONV_DIM] = y_conv.astype(ycat_ref.dtype)
        ycat_ref[:, CONV_DIM:] = z.astype(ycat_ref.dtype)

    @pl.when(i < prompt_steps)
    def _prompt():
        t = i % steps_per_seq

        @pl.when(t == 0)
        def _():
            ev_ref[0:HIST, :] = jnp.zeros((HIST, CONV_DIM), f32)
            eu_ref[0:HIST, :] = jnp.zeros((HIST, POOL_DIM), f32)

        @pl.when(t > 0)
        def _():
            ev_ref[0:HIST, :] = ev_ref[rows:rows + HIST, :]
            eu_ref[0:HIST, :] = eu_ref[rows:rows + HIST, :]

        ev_ref[HIST:HIST + rows, :] = v
        eu_ref[HIST:HIST + rows, :] = u
        pos1 = t * rows + lax.broadcasted_iota(i32, (rows, 1), 0) + 1
        cnts = [jnp.minimum(pos1, w).astype(f32) for w in POOL_WINDOWS]
        finish(conv_taps(rows), [window_sum(g, rows) for g in range(len(POOL_WINDOWS))], cnts)

        @pl.when(t == steps_per_seq - 1)
        def _():
            ncp_ref[0] = ev_ref[HIST + rows - (CONV_W - 1):HIST + rows, :]
            npp_ref[0] = eu_ref[HIST + rows - POOL_BUF:HIST + rows, :]

    @pl.when(i >= prompt_steps)
    def _sample():
        s = seqs_per_step
        new = rows // s
        v3 = v.reshape(s, new, CONV_DIM)
        u3 = u.reshape(s, new, POOL_DIM)
        extu = jnp.concatenate([hu_ref[...], u3], axis=1)
        nv, nu = s * (CONV_HIST + new), s * (HIST + new)
        ev_ref[HIST:HIST + nv, :] = jnp.concatenate([hv_ref[...], v3], axis=1).reshape(nv, CONV_DIM)
        eu_ref[HIST:HIST + nu, :] = extu.reshape(nu, POOL_DIM)
        conv = conv_taps(nv).reshape(s, CONV_HIST + new, CONV_DIM)[:, CONV_HIST:, :].reshape(rows, CONV_DIM)
        sums = [window_sum(g, nu).reshape(s, HIST + new, POOL_GROUP_DIM)[:, HIST:, :].reshape(rows, POOL_GROUP_DIM)
                for g in range(len(POOL_WINDOWS))]
        cnts = [jnp.float32(min(PAST_LEN + 1, w)) for w in POOL_WINDOWS]
        finish(conv, sums, cnts)
        ncs_ref[...] = v3[:, new - (CONV_W - 1):, :]
        nps_ref[...] = extu[:, HIST + new - POOL_BUF:, :]


def _mixer(proj, hv, hu, conv_w, pool_w, pool_scale, layer, *, n_prompt_seq, prompt_len, n_sample_seq, sample_len):
    t_all = proj.shape[0]
    rows = MIX_ROWS
    steps_per_seq = prompt_len // rows
    prompt_steps = n_prompt_seq * steps_per_seq
    seqs_per_step = rows // sample_len
    sample_steps = n_sample_seq // seqs_per_step
    last_p = n_prompt_seq - 1
    n_groups = len(POOL_WINDOWS)

    def samp(i):
        return jnp.maximum(i - prompt_steps, 0)

    def pseq(i):
        return jnp.minimum(i // steps_per_seq, last_p)

    ext_rows = HIST + max(rows + HIST, seqs_per_step * (HIST + sample_len))
    kern = functools.partial(_mixer_kernel, prompt_steps=prompt_steps, steps_per_seq=steps_per_seq,
                             seqs_per_step=seqs_per_step)
    return pl.pallas_call(
        kern,
        grid=(prompt_steps + sample_steps,),
        in_specs=[pl.BlockSpec((rows, IN_DIM), lambda i: (i, 0)),
                  pl.BlockSpec((None, seqs_per_step, CONV_HIST, CONV_DIM), lambda i: (layer, samp(i), 0, 0)),
                  pl.BlockSpec((None, seqs_per_step, HIST, POOL_DIM), lambda i: (layer, samp(i), 0, 0)),
                  pl.BlockSpec((None, CONV_W, CONV_DIM), lambda i: (layer, 0, 0)),
                  pl.BlockSpec((None, n_groups, POOL_GROUP_DIM, POOL_GROUP_DIM), lambda i: (layer, 0, 0, 0)),
                  pl.BlockSpec((None, 1, POOL_DIM), lambda i: (layer, 0, 0))],
        out_specs=[pl.BlockSpec((rows, CONV_DIM + POOL_DIM), lambda i: (i, 0)),
                   pl.BlockSpec((1, CONV_W - 1, CONV_DIM), lambda i: (pseq(i), 0, 0)),
                   pl.BlockSpec((1, POOL_BUF, POOL_DIM), lambda i: (pseq(i), 0, 0)),
                   pl.BlockSpec((seqs_per_step, CONV_W - 1, CONV_DIM), lambda i: (samp(i), 0, 0)),
                   pl.BlockSpec((seqs_per_step, POOL_BUF, POOL_DIM), lambda i: (samp(i), 0, 0))],
        out_shape=[jax.ShapeDtypeStruct((t_all, CONV_DIM + POOL_DIM), bf16),
                   jax.ShapeDtypeStruct((n_prompt_seq, CONV_W - 1, CONV_DIM), f32),
                   jax.ShapeDtypeStruct((n_prompt_seq, POOL_BUF, POOL_DIM), f32),
                   jax.ShapeDtypeStruct((n_sample_seq, CONV_W - 1, CONV_DIM), f32),
                   jax.ShapeDtypeStruct((n_sample_seq, POOL_BUF, POOL_DIM), f32)],
        scratch_shapes=[pltpu.VMEM((ext_rows, CONV_DIM), f32), pltpu.VMEM((ext_rows, POOL_DIM), f32)],
        compiler_params=_params("arbitrary"), name="mixer",
    )(proj, hv, hu, conv_w, pool_w, pool_scale.reshape(pool_scale.shape[0], 1, POOL_DIM))


def kernel(x_prompt, x_sample, state_conv, state_pool, norm_mix, norm_ffn, w_in, conv_w, pool_w, pool_scale,
           w_out, dense_w1, dense_w3, dense_w2, router_w, moe_w1, moe_w3, moe_w2, final_norm):
    n_p, len_p, d = x_prompt.shape
    n_s, len_s, _ = x_sample.shape
    t_p, t_s = n_p * len_p, n_s * len_s
    t = t_p + t_s
    depth = w_in.shape[0]
    n_moe, n_exp = moe_w1.shape[0], moe_w1.shape[1]
    x = jnp.concatenate([x_prompt.reshape(t_p, d), x_sample.reshape(t_s, d)], axis=0)
    hv_all = jnp.pad(state_conv, ((0, 0), (0, 0), (CONV_HIST - (CONV_W - 1), 0), (0, 0)))
    hu_all = jnp.pad(state_pool, ((0, 0), (0, 0), (HIST - POOL_BUF, 0), (0, 0)))
    moe_w1f = moe_w1.reshape((n_moe * n_exp,) + moe_w1.shape[2:])
    moe_w3f = moe_w3.reshape((n_moe * n_exp,) + moe_w3.shape[2:])
    moe_w2f = moe_w2.reshape((n_moe * n_exp,) + moe_w2.shape[2:])

    dense_tiles = t // MLP_ROWS
    dense_blk = jnp.arange(dense_tiles, dtype=i32)
    dense_nsub = jnp.full((dense_tiles,), MLP_NSUB, i32)
    moe_tiles = (TOP_K * t) // MLP_ROWS + n_exp

    conv_p, pool_p, conv_s, pool_s = [], [], [], []
    for l in range(depth):
        xn = _rmsnorm(x, norm_mix[l])
        proj = _matmul(xn, w_in, l, tm=1024, tn=512)
        ycat, ncp, npp, ncs, nps = _mixer(proj, hv_all, hu_all, conv_w, pool_w, pool_scale, l,
                                          n_prompt_seq=n_p, prompt_len=len_p, n_sample_seq=n_s, sample_len=len_s)
        conv_p.append(ncp); pool_p.append(npp); conv_s.append(ncs); pool_s.append(nps)
        h = _matmul(ycat, w_out, l, res=x, tm=1024, tn=512)
        i = l // 2
        if l % 2 == 0:
            hn = _rmsnorm(h, norm_ffn[l])
            x = _mlp(hn, dense_w1, dense_w3, dense_w2, jnp.full((dense_tiles,), i, i32), dense_nsub, dense_blk, res=h)
        else:
            hn, gates, idx = _rmsnorm(h, norm_ffn[l], router_w=router_w[i], out_dtype=f32)
            pos, src, tile_e, tile_nsub, tile_blk = _route_layout(idx, moe_tiles)
            xs = _gather_rows(hn, src, tile_nsub)
            ys = _mlp(xs, moe_w1f, moe_w3f, moe_w2f, tile_e + i * n_exp, tile_nsub, tile_blk)
            x = _combine(h, gates, pos, ys)

    y_prompt = _rmsnorm(x, final_norm, out_dtype=f32, rows=t_p).reshape(n_p, len_p, d)
    y_sample = _rmsnorm(x, final_norm, out_dtype=f32, row_offset=t_p, rows=t_s).reshape(n_s, len_s, d)
    return (y_prompt, y_sample, jnp.stack(conv_p), jnp.stack(pool_p), jnp.stack(conv_s), jnp.stack(pool_s))
```

```python
import functools

import jax
import jax.numpy as jnp
from jax import lax
from jax.experimental import pallas as pl
from jax.experimental.pallas import tpu as pltpu

f32 = jnp.float32
bf16 = jnp.bfloat16
i32 = jnp.int32

CONV_DIM = 1024
POOL_DIM = 1024
IN_DIM = 3 * CONV_DIM + POOL_DIM
CONV_W = 3
POOL_WINDOWS = (2, 4, 8, 16)
POOL_GROUP_DIM = POOL_DIM // len(POOL_WINDOWS)
POOL_BUF = max(POOL_WINDOWS) - 1
N_EXPERTS = 8
TOP_K = 2
PAST_LEN = 16384
EPS = 1e-6

LANES = 128
HIST = 16
CONV_HIST = 8
MIX_ROWS = 256
SAMPLE_SEQS = 16
CAST_ROWS = 512
MLP_ROWS = 1024
MLP_SUB = 256
MLP_NSUB = MLP_ROWS // MLP_SUB
MLP_FC = 256
COMBINE_ROWS = 256
DMA_UNROLL = 8
VMEM_LIMIT = 60000 * 1024


def _params(*sem):
    return pltpu.CompilerParams(dimension_semantics=sem, vmem_limit_bytes=VMEM_LIMIT)


def _rms(x, g):
    return x * lax.rsqrt(jnp.mean(x * x, axis=-1, keepdims=True) + EPS) * g


def _norm_kernel(x_ref, g_ref, o_ref):
    o_ref[...] = _rms(x_ref[...], g_ref[...])


def _rmsnorm(x, g, *, row_offset, rows, tm=512):
    d = x.shape[1]
    off = row_offset // tm
    return pl.pallas_call(
        _norm_kernel, grid=(rows // tm,),
        in_specs=[pl.BlockSpec((tm, d), lambda i: (i + off, 0)), pl.BlockSpec((1, d), lambda i: (0, 0))],
        out_specs=pl.BlockSpec((tm, d), lambda i: (i, 0)),
        out_shape=jax.ShapeDtypeStruct((rows, d), f32),
        compiler_params=_params("arbitrary"), name="rmsnorm",
    )(x, g.reshape(1, d))


def _cast_kernel(w_ref, o_ref):
    o_ref[...] = w_ref[...].astype(o_ref.dtype)


def _to_bf16(w):
    l, k, n = w.shape
    return pl.pallas_call(
        _cast_kernel, grid=(l, k // CAST_ROWS),
        in_specs=[pl.BlockSpec((None, CAST_ROWS, n), lambda a, b: (a, b, 0))],
        out_specs=pl.BlockSpec((None, CAST_ROWS, n), lambda a, b: (a, b, 0)),
        out_shape=jax.ShapeDtypeStruct(w.shape, bf16),
        compiler_params=_params("arbitrary", "arbitrary"), name="cast_bf16",
    )(w)


def _block_kernel(x_ref, gm_ref, win_ref, wout_ref, hv_ref, hu_ref, cw_ref, pw_ref, ps_ref, gf_ref, *rest,
                  with_router, prompt_steps, steps_per_seq, sample_len):
    if with_router:
        rw_ref, h_ref, hn_ref, gates_ref, idx_ref = rest[:5]
        rest = rest[5:]
    else:
        h_ref, hn_ref = rest[:2]
        rest = rest[2:]
    ncp_ref, npp_ref, ncs_ref, nps_ref, proj_ref, ycat_ref, ev_ref, eu_ref = rest
    i = pl.program_id(0)
    rows = MIX_ROWS
    n_groups = len(POOL_WINDOWS)

    xn = _rms(x_ref[...], gm_ref[...]).astype(bf16)
    proj_ref[...] = jnp.dot(xn, win_ref[...], preferred_element_type=f32)

    def gated_v(r0, n):
        return proj_ref[r0:r0 + n, CONV_DIM:2 * CONV_DIM] * proj_ref[r0:r0 + n, 2 * CONV_DIM:3 * CONV_DIM]

    def conv_taps(n):
        conv = cw_ref[0:1, :] * ev_ref[HIST - 2:HIST - 2 + n, :]
        conv = conv + cw_ref[1:2, :] * ev_ref[HIST - 1:HIST - 1 + n, :]
        return conv + cw_ref[2:3, :] * ev_ref[HIST:HIST + n, :]

    def window_sum(g, n):
        lo, hi = g * POOL_GROUP_DIM, (g + 1) * POOL_GROUP_DIM
        s = eu_ref[HIST:HIST + n, lo:hi]
        for j in range(1, POOL_WINDOWS[g]):
            s = s + eu_ref[HIST - j:HIST - j + n, lo:hi]
        return s

    def full_window_mean(g, s):
        w = POOL_WINDOWS[g]
        assert w & (w - 1) == 0, "scaling by 1/w equals dividing by w only for a power of two"
        return s * (1.0 / w)

    def finish(r0, n, conv, sums, mean):
        ycat_ref[r0:r0 + n, 0:CONV_DIM] = (proj_ref[r0:r0 + n, 0:CONV_DIM] * conv).astype(bf16)
        for g in range(n_groups):
            lo, hi = g * POOL_GROUP_DIM, (g + 1) * POOL_GROUP_DIM
            pooled = mean(g, sums[g]) - proj_ref[r0:r0 + n, 3 * CONV_DIM + lo:3 * CONV_DIM + hi]
            z = jnp.dot(pooled.astype(bf16), pw_ref[g].astype(bf16), preferred_element_type=f32)
            ycat_ref[r0:r0 + n, CONV_DIM + lo:CONV_DIM + hi] = (z * ps_ref[:, lo:hi]).astype(bf16)

    @pl.when(i < prompt_steps)
    def _prompt():
        t = i % steps_per_seq

        @pl.when(t == 0)
        def _():
            ev_ref[0:HIST, :] = jnp.zeros((HIST, CONV_DIM), f32)
            eu_ref[0:HIST, :] = jnp.zeros((HIST, POOL_DIM), f32)

        @pl.when(t > 0)
        def _():
            ev_ref[0:HIST, :] = ev_ref[rows:rows + HIST, :]
            eu_ref[0:HIST, :] = eu_ref[rows:rows + HIST, :]

        ev_ref[HIST:HIST + rows, :] = gated_v(0, rows)
        eu_ref[HIST:HIST + rows, :] = proj_ref[:, 3 * CONV_DIM:]
        conv = conv_taps(rows)
        sums = [window_sum(g, rows) for g in range(n_groups)]

        @pl.when(t == 0)
        def _():
            pos1 = lax.broadcasted_iota(i32, (rows, 1), 0) + 1
            finish(0, rows, conv, sums, lambda g, s: s / jnp.minimum(pos1, POOL_WINDOWS[g]).astype(f32))

        @pl.when(t > 0)
        def _():
            assert rows >= max(POOL_WINDOWS)
            finish(0, rows, conv, sums, full_window_mean)

        @pl.when(t == steps_per_seq - 1)
        def _():
            ncp_ref[0] = ev_ref[HIST + rows - (CONV_W - 1):HIST + rows, :]
            npp_ref[0] = eu_ref[HIST + rows - POOL_BUF:HIST + rows, :]

    @pl.when(i >= prompt_steps)
    def _sample():
        s, new = SAMPLE_SEQS, sample_len
        n = s * new
        nv, nu = s * (CONV_HIST + new), s * (HIST + new)
        assert PAST_LEN + 1 >= max(POOL_WINDOWS)
        for part in range(rows // n):
            r0, s0 = part * n, part * s
            v3 = gated_v(r0, n).reshape(s, new, CONV_DIM)
            u3 = proj_ref[r0:r0 + n, 3 * CONV_DIM:].reshape(s, new, POOL_DIM)
            extu = jnp.concatenate([hu_ref[s0:s0 + s], u3], axis=1)
            ev_ref[HIST:HIST + nv, :] = jnp.concatenate([hv_ref[s0:s0 + s], v3], axis=1).reshape(nv, CONV_DIM)
            eu_ref[HIST:HIST + nu, :] = extu.reshape(nu, POOL_DIM)
            conv = conv_taps(nv).reshape(s, CONV_HIST + new, CONV_DIM)[:, CONV_HIST:, :].reshape(n, CONV_DIM)
            sums = [window_sum(g, nu).reshape(s, HIST + new, POOL_GROUP_DIM)[:, HIST:, :].reshape(n, POOL_GROUP_DIM)
                    for g in range(n_groups)]
            finish(r0, n, conv, sums, full_window_mean)
            ncs_ref[s0:s0 + s] = v3[:, new - (CONV_W - 1):, :]
            nps_ref[s0:s0 + s] = extu[:, HIST + new - POOL_BUF:, :]

    h = x_ref[...] + jnp.dot(ycat_ref[...], wout_ref[...], preferred_element_type=f32)
    h_ref[...] = h
    hn = _rms(h, gf_ref[...])
    hn_ref[...] = hn.astype(hn_ref.dtype)
    if with_router:
        rw = rw_ref[...]
        rw_hi = rw.astype(bf16)
        rw_lo = (rw - rw_hi.astype(f32)).astype(bf16)
        hn_hi = hn.astype(bf16)
        hn_lo = (hn - hn_hi.astype(f32)).astype(bf16)
        half = rows // 2
        logits = jnp.concatenate([
            jnp.dot(hn_hi[r0:r0 + half], rw_hi, preferred_element_type=f32)
            + (jnp.dot(hn_lo[r0:r0 + half], rw_hi, preferred_element_type=f32)
               + jnp.dot(hn_hi[r0:r0 + half], rw_lo, preferred_element_type=f32))
            for r0 in (0, half)], axis=0)
        lane = lax.broadcasted_iota(i32, logits.shape, 1)
        neg = jnp.float32(-jnp.inf)
        logits = jnp.where(lane < N_EXPERTS, logits, neg)
        m1 = jnp.max(logits, axis=-1, keepdims=True)
        i1 = jnp.min(jnp.where(logits == m1, lane, LANES), axis=-1, keepdims=True)
        rest_l = jnp.where(lane == i1, neg, logits)
        m2 = jnp.max(rest_l, axis=-1, keepdims=True)
        i2 = jnp.min(jnp.where(rest_l == m2, lane, LANES), axis=-1, keepdims=True)
        e = jnp.exp(m2 - m1)
        denom = 1.0 + e
        gates_ref[...] = jnp.where(lane == 0, 1.0 / denom, jnp.where(lane == 1, e / denom, 0.0))
        idx_ref[...] = jnp.where(lane == 0, i1, jnp.where(lane == 1, i2, 0))


def _block(x, layer, norm_mix, w_in_b, w_out_b, hv, hu, conv_w, pool_w, pool_scale, norm_ffn, router_w, moe_layer,
           *, n_prompt_seq, prompt_len, n_sample_seq, sample_len):
    t_all, d = x.shape
    rows = MIX_ROWS
    steps_per_seq = prompt_len // rows
    prompt_steps = n_prompt_seq * steps_per_seq
    seqs_per_step = rows // sample_len
    sample_steps = n_sample_seq // seqs_per_step
    last_p = n_prompt_seq - 1
    n_groups = len(POOL_WINDOWS)
    with_router = router_w is not None
    once = pl.Buffered(1)

    def samp(i):
        return jnp.maximum(i - prompt_steps, 0)

    def pseq(i):
        return jnp.minimum(i // steps_per_seq, last_p)

    ext_rows = HIST + max(rows, SAMPLE_SEQS * (HIST + sample_len))
    in_specs = [pl.BlockSpec((rows, d), lambda i: (i, 0)),
                pl.BlockSpec((None, 1, d), lambda i: (layer, 0, 0)),
                pl.BlockSpec((None, d, IN_DIM), lambda i: (layer, 0, 0), pipeline_mode=once),
                pl.BlockSpec((None, CONV_DIM + POOL_DIM, d), lambda i: (layer, 0, 0), pipeline_mode=once),
                pl.BlockSpec((None, seqs_per_step, CONV_HIST, CONV_DIM), lambda i: (layer, samp(i), 0, 0),
                             pipeline_mode=once),
                pl.BlockSpec((None, seqs_per_step, HIST, POOL_DIM), lambda i: (layer, samp(i), 0, 0),
                             pipeline_mode=once),
                pl.BlockSpec((None, CONV_W, CONV_DIM), lambda i: (layer, 0, 0)),
                pl.BlockSpec((None, n_groups, POOL_GROUP_DIM, POOL_GROUP_DIM), lambda i: (layer, 0, 0, 0)),
                pl.BlockSpec((None, 1, POOL_DIM), lambda i: (layer, 0, 0)),
                pl.BlockSpec((None, 1, d), lambda i: (layer, 0, 0))]
    args = [x, norm_mix.reshape(-1, 1, d), w_in_b, w_out_b, hv, hu, conv_w, pool_w,
            pool_scale.reshape(-1, 1, POOL_DIM), norm_ffn.reshape(-1, 1, d)]
    out_specs = [pl.BlockSpec((rows, d), lambda i: (i, 0)), pl.BlockSpec((rows, d), lambda i: (i, 0))]
    out_shape = [jax.ShapeDtypeStruct((t_all, d), f32),
                 jax.ShapeDtypeStruct((t_all, d), f32 if with_router else bf16)]
    if with_router:
        in_specs.append(pl.BlockSpec((None, d, LANES), lambda i: (moe_layer, 0, 0)))
        args.append(router_w)
        out_specs += [pl.BlockSpec((rows, LANES), lambda i: (i, 0)), pl.BlockSpec((rows, LANES), lambda i: (i, 0))]
        out_shape += [jax.ShapeDtypeStruct((t_all, LANES), f32), jax.ShapeDtypeStruct((t_all, LANES), i32)]
    out_specs += [pl.BlockSpec((1, CONV_W - 1, CONV_DIM), lambda i: (pseq(i), 0, 0)),
                  pl.BlockSpec((1, POOL_BUF, POOL_DIM), lambda i: (pseq(i), 0, 0)),
                  pl.BlockSpec((seqs_per_step, CONV_W - 1, CONV_DIM), lambda i: (samp(i), 0, 0), pipeline_mode=once),
                  pl.BlockSpec((seqs_per_step, POOL_BUF, POOL_DIM), lambda i: (samp(i), 0, 0), pipeline_mode=once)]
    out_shape += [jax.ShapeDtypeStruct((n_prompt_seq, CONV_W - 1, CONV_DIM), f32),
                  jax.ShapeDtypeStruct((n_prompt_seq, POOL_BUF, POOL_DIM), f32),
                  jax.ShapeDtypeStruct((n_sample_seq, CONV_W - 1, CONV_DIM), f32),
                  jax.ShapeDtypeStruct((n_sample_seq, POOL_BUF, POOL_DIM), f32)]
    kern = functools.partial(_block_kernel, with_router=with_router, prompt_steps=prompt_steps,
                             steps_per_seq=steps_per_seq, sample_len=sample_len)
    return pl.pallas_call(
        kern, grid=(prompt_steps + sample_steps,), in_specs=in_specs, out_specs=out_specs, out_shape=out_shape,
        scratch_shapes=[pltpu.VMEM((rows, IN_DIM), f32), pltpu.VMEM((rows, CONV_DIM + POOL_DIM), bf16),
                        pltpu.VMEM((ext_rows, CONV_DIM), f32), pltpu.VMEM((ext_rows, POOL_DIM), f32)],
        compiler_params=_params("arbitrary"), name="block_router" if with_router else "block",
    )(*args)


def _mlp_kernel(te_ref, ns_ref, tb_ref, x_ref, w1_ref, w3_ref, w2_ref, *rest, grouped):
    del te_ref, tb_ref
    if grouped:
        o_ref, xb_ref, w1b_ref, w3b_ref, w2b_ref = rest
    else:
        res_ref, o_ref, w1b_ref, w3b_ref, w2b_ref = rest
    g, c = pl.program_id(0), pl.program_id(1)
    nsub = ns_ref[g]
    xsrc = xb_ref if grouped else x_ref

    def cast_weights():
        w1b_ref[...] = w1_ref[...].astype(bf16)
        w3b_ref[...] = w3_ref[...].astype(bf16)
        w2b_ref[...] = w2_ref[...].astype(bf16)

    def chunk(x):
        h1 = jnp.dot(x, w1b_ref[...], preferred_element_type=f32)
        h3 = jnp.dot(x, w3b_ref[...], preferred_element_type=f32)
        hid = (h1 * jax.nn.sigmoid(h1) * h3).astype(bf16)
        return jnp.dot(hid, w2b_ref[...], preferred_element_type=f32)

    @pl.when(c == 0)
    def _init():
        if grouped:
            o_ref[...] = jnp.zeros_like(o_ref)

            @pl.when(nsub > 0)
            def _():
                xb_ref[...] = x_ref[...].astype(bf16)
        else:
            o_ref[...] = res_ref[...]

    @pl.when(nsub == MLP_NSUB)
    def _full():
        cast_weights()
        o_ref[...] += chunk(xsrc[...])

    if grouped:
        @pl.when((nsub > 0) & (nsub < MLP_NSUB))
        def _partial():
            cast_weights()

            def body(k, carry):
                r = pl.multiple_of(k * MLP_SUB, MLP_SUB)
                o_ref[pl.ds(r, MLP_SUB), :] += chunk(xsrc[pl.ds(r, MLP_SUB), :])
                return carry

            lax.fori_loop(0, nsub, body, 0)


def _mlp(x, w1, w3, w2, tile_expert, tile_nsub, tile_blk, *, res=None):
    grouped = res is None
    k = x.shape[1]
    f, n = w2.shape[1], w2.shape[2]
    n_tiles = tile_expert.shape[0]
    n_chunks = f // MLP_FC

    def wcol(g, c, te, ns, tb):
        return jnp.where(ns[g] > 0, c, n_chunks - 1)

    in_specs = [
        pl.BlockSpec((MLP_ROWS, k), lambda g, c, te, ns, tb: (tb[g], 0)),
        pl.BlockSpec((None, k, MLP_FC), lambda g, c, te, ns, tb: (te[g], 0, wcol(g, c, te, ns, tb))),
        pl.BlockSpec((None, k, MLP_FC), lambda g, c, te, ns, tb: (te[g], 0, wcol(g, c, te, ns, tb))),
        pl.BlockSpec((None, MLP_FC, n), lambda g, c, te, ns, tb: (te[g], wcol(g, c, te, ns, tb), 0)),
    ]
    args = [x, w1, w3, w2]
    scratch = [pltpu.VMEM((k, MLP_FC), bf16), pltpu.VMEM((k, MLP_FC), bf16), pltpu.VMEM((MLP_FC, n), bf16)]
    if grouped:
        scratch = [pltpu.VMEM((MLP_ROWS, k), bf16)] + scratch
    else:
        in_specs.append(pl.BlockSpec((MLP_ROWS, n), lambda g, c, te, ns, tb: (tb[g], 0),
                                     pipeline_mode=pl.Buffered(1)))
        args.append(res)
    return pl.pallas_call(
        functools.partial(_mlp_kernel, grouped=grouped),
        grid_spec=pltpu.PrefetchScalarGridSpec(
            num_scalar_prefetch=3, grid=(n_tiles, n_chunks), in_specs=in_specs,
            out_specs=pl.BlockSpec((MLP_ROWS, n), lambda g, c, te, ns, tb: (g, 0),
                                   pipeline_mode=pl.Buffered(1)),
            scratch_shapes=scratch),
        out_shape=jax.ShapeDtypeStruct((x.shape[0], n), f32),
        compiler_params=_params("arbitrary", "arbitrary"), name="mlp_grouped" if grouped else "mlp_dense",
    )(tile_expert, tile_nsub, tile_blk, *args)


def _route_layout(idx, n_tiles):
    t = idx.shape[0]
    flat_e = idx[:, :TOP_K].reshape(-1)
    onehot = (flat_e[:, None] == jnp.arange(N_EXPERTS, dtype=i32)[None, :]).astype(i32)
    csum = jnp.cumsum(onehot, axis=0)
    rank = jnp.sum(onehot * csum, axis=1) - 1
    cnt = csum[-1]
    tiles_e = (cnt + MLP_ROWS - 1) // MLP_ROWS
    tile_end = jnp.cumsum(tiles_e)
    tile_start = tile_end - tiles_e
    pos = jnp.sum(onehot * tile_start[None, :], axis=1) * MLP_ROWS + rank
    n_used = tile_end[-1]
    g = jnp.arange(n_tiles, dtype=i32)
    g_eff = jnp.minimum(g, n_used - 1)
    tile_e = jnp.minimum(jnp.sum((tile_end[None, :] <= g_eff[:, None]).astype(i32), axis=1), N_EXPERTS - 1)
    tile_oh = (tile_e[:, None] == jnp.arange(N_EXPERTS, dtype=i32)[None, :]).astype(i32)
    rows_in = jnp.clip(jnp.sum(tile_oh * cnt[None, :], axis=1)
                       - (g_eff - jnp.sum(tile_oh * tile_start[None, :], axis=1)) * MLP_ROWS, 0, MLP_ROWS)
    nsub = jnp.where(g < n_used, (rows_in + MLP_SUB - 1) // MLP_SUB, 0).astype(i32)
    src = jnp.zeros((n_tiles * MLP_ROWS,), i32).at[pos].set(jnp.arange(TOP_K * t, dtype=i32) // TOP_K)
    return pos.reshape(t, TOP_K), src, tile_e, nsub, g_eff.astype(i32)


def _row_copy(src_hbm, dst_ref, src_row, dst_row, sem):
    return pltpu.make_async_copy(src_hbm.at[pl.ds(src_row, 1)], dst_ref.at[pl.ds(dst_row, 1)], sem)


def _gather_kernel(ns_ref, src_ref, x_hbm, o_ref, sem):
    g = pl.program_id(0)
    nsub = ns_ref[g]
    groups = nsub * (MLP_SUB // DMA_UNROLL)

    def issue(q, carry):
        for u in range(DMA_UNROLL):
            r = q * DMA_UNROLL + u
            _row_copy(x_hbm, o_ref, src_ref[0, 0, r], r, sem).start()
        return carry

    def drain(q, carry):
        for u in range(DMA_UNROLL):
            _row_copy(x_hbm, o_ref, 0, 0, sem).wait()
        return carry

    lax.fori_loop(0, groups, issue, 0)
    lax.fori_loop(0, groups, drain, 0)
    for k in range(MLP_NSUB):
        @pl.when(nsub <= k)
        def _():
            o_ref[k * MLP_SUB:(k + 1) * MLP_SUB, :] = jnp.zeros((MLP_SUB, o_ref.shape[1]), o_ref.dtype)


def _gather_rows(x, src, tile_nsub):
    n_tiles = tile_nsub.shape[0]
    d = x.shape[1]
    return pl.pallas_call(
        _gather_kernel,
        grid_spec=pltpu.PrefetchScalarGridSpec(
            num_scalar_prefetch=1, grid=(n_tiles,),
            in_specs=[pl.BlockSpec((1, 1, MLP_ROWS), lambda g, ns: (g, 0, 0), memory_space=pltpu.SMEM),
                      pl.BlockSpec(memory_space=pl.ANY)],
            out_specs=pl.BlockSpec((MLP_ROWS, d), lambda g, ns: (g, 0)),
            scratch_shapes=[pltpu.SemaphoreType.DMA(())]),
        out_shape=jax.ShapeDtypeStruct((n_tiles * MLP_ROWS, d), x.dtype),
        compiler_params=_params("arbitrary"), name="moe_gather",
    )(tile_nsub, src.reshape(n_tiles, 1, MLP_ROWS), x)


def _combine_kernel(pos_ref, h_ref, gates_ref, y_hbm, o_ref, ya_ref, yb_ref, sem):
    def issue(q, carry):
        for u in range(DMA_UNROLL):
            r = q * DMA_UNROLL + u
            _row_copy(y_hbm, ya_ref, pos_ref[0, 0, TOP_K * r], r, sem.at[0]).start()
            _row_copy(y_hbm, yb_ref, pos_ref[0, 0, TOP_K * r + 1], r, sem.at[1]).start()
        return carry

    def drain(q, carry):
        for u in range(DMA_UNROLL):
            _row_copy(y_hbm, ya_ref, 0, 0, sem.at[0]).wait()
            _row_copy(y_hbm, yb_ref, 0, 0, sem.at[1]).wait()
        return carry

    lax.fori_loop(0, COMBINE_ROWS // DMA_UNROLL, issue, 0)
    lax.fori_loop(0, COMBINE_ROWS // DMA_UNROLL, drain, 0)
    gates = gates_ref[...]
    o_ref[...] = h_ref[...] + (gates[:, 0:1] * ya_ref[...] + gates[:, 1:2] * yb_ref[...])


def _combine(h, gates, pos, y):
    t, d = h.shape
    rows = COMBINE_ROWS
    return pl.pallas_call(
        _combine_kernel,
        grid=(t // rows,),
        in_specs=[pl.BlockSpec((1, 1, TOP_K * rows), lambda i: (i, 0, 0), memory_space=pltpu.SMEM),
                  pl.BlockSpec((rows, d), lambda i: (i, 0)),
                  pl.BlockSpec((rows, LANES), lambda i: (i, 0)),
                  pl.BlockSpec(memory_space=pl.ANY)],
        out_specs=pl.BlockSpec((rows, d), lambda i: (i, 0)),
        out_shape=jax.ShapeDtypeStruct((t, d), f32),
        scratch_shapes=[pltpu.VMEM((rows, d), f32), pltpu.VMEM((rows, d), f32), pltpu.SemaphoreType.DMA((2,))],
        compiler_params=_params("arbitrary"), name="moe_combine",
    )(pos.reshape(t // rows, 1, TOP_K * rows), h, gates, y)


def kernel(x_prompt, x_sample, state_conv, state_pool, norm_mix, norm_ffn, w_in, conv_w, pool_w, pool_scale,
           w_out, dense_w1, dense_w3, dense_w2, router_w, moe_w1, moe_w3, moe_w2, final_norm):
    n_p, len_p, d = x_prompt.shape
    n_s, len_s, _ = x_sample.shape
    t_p, t_s = n_p * len_p, n_s * len_s
    t = t_p + t_s
    depth = w_in.shape[0]
    n_moe, n_exp = moe_w1.shape[0], moe_w1.shape[1]
    x = jnp.concatenate([x_prompt.reshape(t_p, d), x_sample.reshape(t_s, d)], axis=0)
    hv_all = jnp.pad(state_conv, ((0, 0), (0, 0), (CONV_HIST - (CONV_W - 1), 0), (0, 0)))
    hu_all = jnp.pad(state_pool, ((0, 0), (0, 0), (HIST - POOL_BUF, 0), (0, 0)))
    router_pad = jnp.pad(router_w, ((0, 0), (0, 0), (0, LANES - router_w.shape[2])))
    moe_w1f = moe_w1.reshape((n_moe * n_exp,) + moe_w1.shape[2:])
    moe_w3f = moe_w3.reshape((n_moe * n_exp,) + moe_w3.shape[2:])
    moe_w2f = moe_w2.reshape((n_moe * n_exp,) + moe_w2.shape[2:])
    w_in_b, w_out_b = _to_bf16(w_in), _to_bf16(w_out)

    dense_tiles = t // MLP_ROWS
    dense_blk = jnp.arange(dense_tiles, dtype=i32)
    dense_nsub = jnp.full((dense_tiles,), MLP_NSUB, i32)
    moe_tiles = (TOP_K * t) // MLP_ROWS + n_exp

    conv_p, pool_p, conv_s, pool_s = [], [], [], []
    for l in range(depth):
        i = l // 2
        is_moe = l % 2 == 1
        outs = _block(x, l, norm_mix, w_in_b, w_out_b, hv_all, hu_all, conv_w, pool_w, pool_scale, norm_ffn,
                      router_pad if is_moe else None, i,
                      n_prompt_seq=n_p, prompt_len=len_p, n_sample_seq=n_s, sample_len=len_s)
        conv_p.append(outs[-4]); pool_p.append(outs[-3]); conv_s.append(outs[-2]); pool_s.append(outs[-1])
        if is_moe:
            h, hn, gates, idx = outs[:4]
            pos, src, tile_e, tile_nsub, tile_blk = _route_layout(idx, moe_tiles)
            xs = _gather_rows(hn, src, tile_nsub)
            ys = _mlp(xs, moe_w1f, moe_w3f, moe_w2f, tile_e + i * n_exp, tile_nsub, tile_blk)
            x = _combine(h, gates, pos, ys)
        else:
            h, hn = outs[:2]
            x = _mlp(hn, dense_w1, dense_w3, dense_w2, jnp.full((dense_tiles,), i, i32), dense_nsub, dense_blk, res=h)

    y_prompt = _rmsnorm(x, final_norm, row_offset=0, rows=t_p).reshape(n_p, len_p, d)
    y_sample = _rmsnorm(x, final_norm, row_offset=t_p, rows=t_s).reshape(n_s, len_s, d)
    return (y_prompt, y_sample, jnp.stack(conv_p), jnp.stack(pool_p), jnp.stack(conv_s), jnp.stack(pool_s))
```

```python
import functools

import jax
import jax.numpy as jnp
from jax import lax
from jax.experimental import pallas as pl
from jax.experimental.pallas import tpu as pltpu

f32 = jnp.float32
bf16 = jnp.bfloat16
i32 = jnp.int32

CONV_DIM = 1024
POOL_DIM = 1024
IN_DIM = 3 * CONV_DIM + POOL_DIM
CONV_W = 3
POOL_WINDOWS = (2, 4, 8, 16)
POOL_GROUP_DIM = POOL_DIM // len(POOL_WINDOWS)
POOL_BUF = max(POOL_WINDOWS) - 1
N_EXPERTS = 8
TOP_K = 2
PAST_LEN = 16384
EPS = 1e-6

LANES = 128
HIST = 16
CONV_HIST = 8
MIX_ROWS = 256
SAMPLE_SEQS = 8
CAST_ROWS = 512
MLP_ROWS = 1024
MLP_SUB = 256
MLP_NSUB = MLP_ROWS // MLP_SUB
MLP_FC = 256
MLP_FC_WIDE = 512
COMBINE_ROWS = 256
DMA_UNROLL = 8
VMEM_LIMIT = 60000 * 1024


def _params(*sem):
    return pltpu.CompilerParams(dimension_semantics=sem, vmem_limit_bytes=VMEM_LIMIT)


def _rms(x, g):
    return x * lax.rsqrt(jnp.mean(x * x, axis=-1, keepdims=True) + EPS) * g


def _norm_kernel(x_ref, g_ref, o_ref):
    o_ref[...] = _rms(x_ref[...], g_ref[...])


def _rmsnorm(x, g, *, row_offset, rows, tm=512):
    d = x.shape[1]
    off = row_offset // tm
    return pl.pallas_call(
        _norm_kernel, grid=(rows // tm,),
        in_specs=[pl.BlockSpec((tm, d), lambda i: (i + off, 0)), pl.BlockSpec((1, d), lambda i: (0, 0))],
        out_specs=pl.BlockSpec((tm, d), lambda i: (i, 0)),
        out_shape=jax.ShapeDtypeStruct((rows, d), f32),
        compiler_params=_params("arbitrary"), name="rmsnorm",
    )(x, g.reshape(1, d))


def _cast_kernel(w_ref, o_ref):
    o_ref[...] = w_ref[...].astype(o_ref.dtype)


def _to_bf16(w):
    l, k, n = w.shape
    return pl.pallas_call(
        _cast_kernel, grid=(l, k // CAST_ROWS),
        in_specs=[pl.BlockSpec((None, CAST_ROWS, n), lambda a, b: (a, b, 0))],
        out_specs=pl.BlockSpec((None, CAST_ROWS, n), lambda a, b: (a, b, 0)),
        out_shape=jax.ShapeDtypeStruct(w.shape, bf16),
        compiler_params=_params("arbitrary", "arbitrary"), name="cast_bf16",
    )(w)


def _block_kernel(x_ref, gm_ref, win_ref, wout_ref, hv_ref, hu_ref, cw_ref, pw_ref, ps_ref, gf_ref, *rest,
                  with_router, prompt_steps, steps_per_seq, sample_len):
    if with_router:
        rw_ref, h_ref, hn_ref, gates_ref, idx_ref = rest[:5]
        rest = rest[5:]
    else:
        h_ref, hn_ref = rest[:2]
        rest = rest[2:]
    ncp_ref, npp_ref, ncs_ref, nps_ref, proj_ref, ycat_ref, ev_ref, eu_ref, ev3_ref, eu3_ref = rest
    i = pl.program_id(0)
    rows = MIX_ROWS
    n_groups = len(POOL_WINDOWS)

    xn = _rms(x_ref[...], gm_ref[...]).astype(bf16)
    proj_ref[...] = jnp.dot(xn, win_ref[...], preferred_element_type=f32)

    def gated_v(r0, n):
        return proj_ref[r0:r0 + n, CONV_DIM:2 * CONV_DIM] * proj_ref[r0:r0 + n, 2 * CONV_DIM:3 * CONV_DIM]

    def conv_taps(ext):
        conv = cw_ref[0:1, :] * pltpu.roll(ext, 2, axis=0)
        conv = conv + cw_ref[1:2, :] * pltpu.roll(ext, 1, axis=0)
        return conv + cw_ref[2:3, :] * ext

    def window_sum(ext, w):
        s, k = ext, 1
        while k < w:
            s = s + pltpu.roll(s, k, axis=0)
            k *= 2
        return s

    def full_window_mean(g, s):
        w = POOL_WINDOWS[g]
        assert w & (w - 1) == 0, "scaling by 1/w equals dividing by w only for a power of two"
        return s * (1.0 / w)

    def finish(r0, n, conv, sums, mean):
        ycat_ref[r0:r0 + n, 0:CONV_DIM] = (proj_ref[r0:r0 + n, 0:CONV_DIM] * conv).astype(bf16)
        for g in range(n_groups):
            lo, hi = g * POOL_GROUP_DIM, (g + 1) * POOL_GROUP_DIM
            pooled = mean(g, sums[g]) - proj_ref[r0:r0 + n, 3 * CONV_DIM + lo:3 * CONV_DIM + hi]
            z = jnp.dot(pooled.astype(bf16), pw_ref[g].astype(bf16), preferred_element_type=f32)
            ycat_ref[r0:r0 + n, CONV_DIM + lo:CONV_DIM + hi] = (z * ps_ref[:, lo:hi]).astype(bf16)

    @pl.when(i < prompt_steps)
    def _prompt():
        t = i % steps_per_seq

        @pl.when(t == 0)
        def _():
            ev_ref[0:HIST, :] = jnp.zeros((HIST, CONV_DIM), f32)
            eu_ref[0:HIST, :] = jnp.zeros((HIST, POOL_DIM), f32)

        @pl.when(t > 0)
        def _():
            ev_ref[0:HIST, :] = ev_ref[rows:rows + HIST, :]
            eu_ref[0:HIST, :] = eu_ref[rows:rows + HIST, :]

        ev_ref[HIST:HIST + rows, :] = gated_v(0, rows)
        eu_ref[HIST:HIST + rows, :] = proj_ref[:, 3 * CONV_DIM:]
        conv = conv_taps(ev_ref[...])[HIST:, :]
        sums = [window_sum(eu_ref[:, g * POOL_GROUP_DIM:(g + 1) * POOL_GROUP_DIM], POOL_WINDOWS[g])[HIST:, :]
                for g in range(n_groups)]

        @pl.when(t == 0)
        def _():
            pos1 = lax.broadcasted_iota(i32, (rows, 1), 0) + 1
            finish(0, rows, conv, sums, lambda g, s: s / jnp.minimum(pos1, POOL_WINDOWS[g]).astype(f32))

        @pl.when(t > 0)
        def _():
            assert rows >= max(POOL_WINDOWS)
            finish(0, rows, conv, sums, full_window_mean)

        @pl.when(t == steps_per_seq - 1)
        def _():
            ncp_ref[0] = ev_ref[HIST + rows - (CONV_W - 1):HIST + rows, :]
            npp_ref[0] = eu_ref[HIST + rows - POOL_BUF:HIST + rows, :]

    @pl.when(i >= prompt_steps)
    def _sample():
        s, new = SAMPLE_SEQS, sample_len
        n = s * new
        nv, nu = s * (CONV_HIST + new), s * (HIST + new)
        assert PAST_LEN + 1 >= max(POOL_WINDOWS)
        assert new >= CONV_W - 1
        for part in range(rows // n):
            r0, s0 = part * n, part * s
            v3 = gated_v(r0, n).reshape(s, new, CONV_DIM)
            ev3_ref[:, 0:CONV_HIST - (CONV_W - 1), :] = jnp.zeros((s, CONV_HIST - (CONV_W - 1), CONV_DIM), f32)
            ev3_ref[:, CONV_HIST - (CONV_W - 1):CONV_HIST, :] = hv_ref[s0:s0 + s]
            ev3_ref[:, CONV_HIST:, :] = v3
            eu3_ref[:, 0:HIST - POOL_BUF, :] = jnp.zeros((s, HIST - POOL_BUF, POOL_DIM), f32)
            eu3_ref[:, HIST - POOL_BUF:HIST, :] = hu_ref[s0:s0 + s]
            eu3_ref[:, HIST:, :] = proj_ref[r0:r0 + n, 3 * CONV_DIM:].reshape(s, new, POOL_DIM)
            conv = conv_taps(ev3_ref[...].reshape(nv, CONV_DIM))
            conv = conv.reshape(s, CONV_HIST + new, CONV_DIM)[:, CONV_HIST:, :].reshape(n, CONV_DIM)
            sums = []
            for g in range(n_groups):
                lo, hi = g * POOL_GROUP_DIM, (g + 1) * POOL_GROUP_DIM
                sg = window_sum(eu3_ref[:, :, lo:hi].reshape(nu, POOL_GROUP_DIM), POOL_WINDOWS[g])
                sums.append(sg.reshape(s, HIST + new, POOL_GROUP_DIM)[:, HIST:, :].reshape(n, POOL_GROUP_DIM))
            finish(r0, n, conv, sums, full_window_mean)
            ncs_ref[s0:s0 + s] = v3[:, new - (CONV_W - 1):, :]
            nps_ref[s0:s0 + s] = eu3_ref[:, HIST + new - POOL_BUF:, :]

    h = x_ref[...] + jnp.dot(ycat_ref[...], wout_ref[...], preferred_element_type=f32)
    h_ref[...] = h
    hn = _rms(h, gf_ref[...])
    hn_ref[...] = hn.astype(hn_ref.dtype)
    if with_router:
        rw = rw_ref[...]
        rw_hi = rw.astype(bf16)
        rw_lo = (rw - rw_hi.astype(f32)).astype(bf16)
        hn_hi = hn.astype(bf16)
        hn_lo = (hn - hn_hi.astype(f32)).astype(bf16)
        half = rows // 2
        logits = jnp.concatenate([
            jnp.dot(hn_hi[r0:r0 + half], rw_hi, preferred_element_type=f32)
            + (jnp.dot(hn_lo[r0:r0 + half], rw_hi, preferred_element_type=f32)
               + jnp.dot(hn_hi[r0:r0 + half], rw_lo, preferred_element_type=f32))
            for r0 in (0, half)], axis=0)
        lane = lax.broadcasted_iota(i32, logits.shape, 1)
        neg = jnp.float32(-jnp.inf)
        logits = jnp.where(lane < N_EXPERTS, logits, neg)
        m1 = jnp.max(logits, axis=-1, keepdims=True)
        i1 = jnp.min(jnp.where(logits == m1, lane, LANES), axis=-1, keepdims=True)
        rest_l = jnp.where(lane == i1, neg, logits)
        m2 = jnp.max(rest_l, axis=-1, keepdims=True)
        i2 = jnp.min(jnp.where(rest_l == m2, lane, LANES), axis=-1, keepdims=True)
        e = jnp.exp(m2 - m1)
        denom = 1.0 + e
        gates_ref[...] = jnp.where(lane == 0, 1.0 / denom, jnp.where(lane == 1, e / denom, 0.0))
        idx_ref[...] = jnp.where(lane == 0, i1, jnp.where(lane == 1, i2, 0))


def _block(x, layer, norm_mix, w_in_b, w_out_b, hv, hu, conv_w, pool_w, pool_scale, norm_ffn, router_w, moe_layer,
           *, n_prompt_seq, prompt_len, n_sample_seq, sample_len):
    t_all, d = x.shape
    rows = MIX_ROWS
    steps_per_seq = prompt_len // rows
    prompt_steps = n_prompt_seq * steps_per_seq
    seqs_per_step = rows // sample_len
    sample_steps = n_sample_seq // seqs_per_step
    last_p = n_prompt_seq - 1
    n_groups = len(POOL_WINDOWS)
    with_router = router_w is not None
    once = pl.Buffered(1)

    def samp(i):
        return jnp.maximum(i - prompt_steps, 0)

    def pseq(i):
        return jnp.minimum(i // steps_per_seq, last_p)

    in_specs = [pl.BlockSpec((rows, d), lambda i: (i, 0)),
                pl.BlockSpec((None, 1, d), lambda i: (layer, 0, 0)),
                pl.BlockSpec((None, d, IN_DIM), lambda i: (layer, 0, 0), pipeline_mode=once),
                pl.BlockSpec((None, CONV_DIM + POOL_DIM, d), lambda i: (layer, 0, 0), pipeline_mode=once),
                pl.BlockSpec((None, seqs_per_step, CONV_W - 1, CONV_DIM), lambda i: (layer, samp(i), 0, 0),
                             pipeline_mode=once),
                pl.BlockSpec((None, seqs_per_step, POOL_BUF, POOL_DIM), lambda i: (layer, samp(i), 0, 0),
                             pipeline_mode=once),
                pl.BlockSpec((None, CONV_W, CONV_DIM), lambda i: (layer, 0, 0)),
                pl.BlockSpec((None, n_groups, POOL_GROUP_DIM, POOL_GROUP_DIM), lambda i: (layer, 0, 0, 0)),
                pl.BlockSpec((None, 1, POOL_DIM), lambda i: (layer, 0, 0)),
                pl.BlockSpec((None, 1, d), lambda i: (layer, 0, 0))]
    args = [x, norm_mix.reshape(-1, 1, d), w_in_b, w_out_b, hv, hu, conv_w, pool_w,
            pool_scale.reshape(-1, 1, POOL_DIM), norm_ffn.reshape(-1, 1, d)]
    out_specs = [pl.BlockSpec((rows, d), lambda i: (i, 0)), pl.BlockSpec((rows, d), lambda i: (i, 0))]
    out_shape = [jax.ShapeDtypeStruct((t_all, d), f32),
                 jax.ShapeDtypeStruct((t_all, d), f32 if with_router else bf16)]
    if with_router:
        in_specs.append(pl.BlockSpec((None, d, LANES), lambda i: (moe_layer, 0, 0)))
        args.append(router_w)
        out_specs += [pl.BlockSpec((rows, LANES), lambda i: (i, 0)), pl.BlockSpec((rows, LANES), lambda i: (i, 0))]
        out_shape += [jax.ShapeDtypeStruct((t_all, LANES), f32), jax.ShapeDtypeStruct((t_all, LANES), i32)]
    out_specs += [pl.BlockSpec((1, CONV_W - 1, CONV_DIM), lambda i: (pseq(i), 0, 0)),
                  pl.BlockSpec((1, POOL_BUF, POOL_DIM), lambda i: (pseq(i), 0, 0)),
                  pl.BlockSpec((seqs_per_step, CONV_W - 1, CONV_DIM), lambda i: (samp(i), 0, 0), pipeline_mode=once),
                  pl.BlockSpec((seqs_per_step, POOL_BUF, POOL_DIM), lambda i: (samp(i), 0, 0), pipeline_mode=once)]
    out_shape += [jax.ShapeDtypeStruct((n_prompt_seq, CONV_W - 1, CONV_DIM), f32),
                  jax.ShapeDtypeStruct((n_prompt_seq, POOL_BUF, POOL_DIM), f32),
                  jax.ShapeDtypeStruct((n_sample_seq, CONV_W - 1, CONV_DIM), f32),
                  jax.ShapeDtypeStruct((n_sample_seq, POOL_BUF, POOL_DIM), f32)]
    kern = functools.partial(_block_kernel, with_router=with_router, prompt_steps=prompt_steps,
                             steps_per_seq=steps_per_seq, sample_len=sample_len)
    return pl.pallas_call(
        kern, grid=(prompt_steps + sample_steps,), in_specs=in_specs, out_specs=out_specs, out_shape=out_shape,
        scratch_shapes=[pltpu.VMEM((rows, IN_DIM), f32), pltpu.VMEM((rows, CONV_DIM + POOL_DIM), bf16),
                        pltpu.VMEM((HIST + rows, CONV_DIM), f32), pltpu.VMEM((HIST + rows, POOL_DIM), f32),
                        pltpu.VMEM((SAMPLE_SEQS, CONV_HIST + sample_len, CONV_DIM), f32),
                        pltpu.VMEM((SAMPLE_SEQS, HIST + sample_len, POOL_DIM), f32)],
        compiler_params=_params("arbitrary"), name="block_router" if with_router else "block",
    )(*args)


def _mlp_kernel(te_ref, ns_ref, tb_ref, x_ref, w1_ref, w3_ref, w2_ref, *rest, grouped):
    del te_ref, tb_ref
    if grouped:
        o_ref, xb_ref, w1b_ref, w3b_ref, w2b_ref = rest
    else:
        res_ref, o_ref, w1b_ref, w3b_ref, w2b_ref = rest
    g, c = pl.program_id(0), pl.program_id(1)
    nsub = ns_ref[g]
    xsrc = xb_ref if grouped else x_ref

    def cast_weights():
        w1b_ref[...] = w1_ref[...].astype(bf16)
        w3b_ref[...] = w3_ref[...].astype(bf16)
        w2b_ref[...] = w2_ref[...].astype(bf16)

    def chunk(x):
        h1 = jnp.dot(x, w1b_ref[...], preferred_element_type=f32)
        h3 = jnp.dot(x, w3b_ref[...], preferred_element_type=f32)
        hid = (h1 * jax.nn.sigmoid(h1) * h3).astype(bf16)
        return jnp.dot(hid, w2b_ref[...], preferred_element_type=f32)

    @pl.when(c == 0)
    def _init():
        if grouped:
            o_ref[...] = jnp.zeros_like(o_ref)

            @pl.when(nsub > 0)
            def _():
                xb_ref[...] = x_ref[...].astype(bf16)
        else:
            o_ref[...] = res_ref[...]

    @pl.when(nsub == MLP_NSUB)
    def _full():
        cast_weights()
        o_ref[...] += chunk(xsrc[...])

    if grouped:
        @pl.when((nsub > 0) & (nsub < MLP_NSUB))
        def _partial():
            cast_weights()

            def body(k, carry):
                r = pl.multiple_of(k * MLP_SUB, MLP_SUB)
                o_ref[pl.ds(r, MLP_SUB), :] += chunk(xsrc[pl.ds(r, MLP_SUB), :])
                return carry

            lax.fori_loop(0, nsub, body, 0)


def _mlp(x, w1, w3, w2, tile_expert, tile_nsub, tile_blk, *, res=None):
    grouped = res is None
    k = x.shape[1]
    f, n = w2.shape[1], w2.shape[2]
    n_tiles = tile_expert.shape[0]
    fc = MLP_FC_WIDE if f % MLP_FC_WIDE == 0 else MLP_FC
    n_chunks = f // fc

    def wcol(g, c, te, ns, tb):
        return jnp.where(ns[g] > 0, c, n_chunks - 1)

    in_specs = [
        pl.BlockSpec((MLP_ROWS, k), lambda g, c, te, ns, tb: (tb[g], 0),
                     pipeline_mode=pl.Buffered(2 if grouped else 1)),
        pl.BlockSpec((None, k, fc), lambda g, c, te, ns, tb: (te[g], 0, wcol(g, c, te, ns, tb))),
        pl.BlockSpec((None, k, fc), lambda g, c, te, ns, tb: (te[g], 0, wcol(g, c, te, ns, tb))),
        pl.BlockSpec((None, fc, n), lambda g, c, te, ns, tb: (te[g], wcol(g, c, te, ns, tb), 0)),
    ]
    args = [x, w1, w3, w2]
    scratch = [pltpu.VMEM((k, fc), bf16), pltpu.VMEM((k, fc), bf16), pltpu.VMEM((fc, n), bf16)]
    if grouped:
        scratch = [pltpu.VMEM((MLP_ROWS, k), bf16)] + scratch
    else:
        in_specs.append(pl.BlockSpec((MLP_ROWS, n), lambda g, c, te, ns, tb: (tb[g], 0),
                                     pipeline_mode=pl.Buffered(1)))
        args.append(res)
    return pl.pallas_call(
        functools.partial(_mlp_kernel, grouped=grouped),
        grid_spec=pltpu.PrefetchScalarGridSpec(
            num_scalar_prefetch=3, grid=(n_tiles, n_chunks), in_specs=in_specs,
            out_specs=pl.BlockSpec((MLP_ROWS, n), lambda g, c, te, ns, tb: (g, 0),
                                   pipeline_mode=pl.Buffered(1)),
            scratch_shapes=scratch),
        out_shape=jax.ShapeDtypeStruct((x.shape[0], n), f32),
        compiler_params=_params("arbitrary", "arbitrary"), name="mlp_grouped" if grouped else "mlp_dense",
    )(tile_expert, tile_nsub, tile_blk, *args)


def _route_layout(idx, n_tiles):
    t = idx.shape[0]
    flat_e = idx[:, :TOP_K].reshape(-1)
    onehot = (flat_e[:, None] == jnp.arange(N_EXPERTS, dtype=i32)[None, :]).astype(i32)
    csum = jnp.cumsum(onehot, axis=0)
    rank = jnp.sum(onehot * csum, axis=1) - 1
    cnt = csum[-1]
    tiles_e = (cnt + MLP_ROWS - 1) // MLP_ROWS
    tile_end = jnp.cumsum(tiles_e)
    tile_start = tile_end - tiles_e
    pos = jnp.sum(onehot * tile_start[None, :], axis=1) * MLP_ROWS + rank
    n_used = tile_end[-1]
    g = jnp.arange(n_tiles, dtype=i32)
    g_eff = jnp.minimum(g, n_used - 1)
    tile_e = jnp.minimum(jnp.sum((tile_end[None, :] <= g_eff[:, None]).astype(i32), axis=1), N_EXPERTS - 1)
    tile_oh = (tile_e[:, None] == jnp.arange(N_EXPERTS, dtype=i32)[None, :]).astype(i32)
    rows_in = jnp.clip(jnp.sum(tile_oh * cnt[None, :], axis=1)
                       - (g_eff - jnp.sum(tile_oh * tile_start[None, :], axis=1)) * MLP_ROWS, 0, MLP_ROWS)
    nsub = jnp.where(g < n_used, (rows_in + MLP_SUB - 1) // MLP_SUB, 0).astype(i32)
    src = jnp.zeros((n_tiles * MLP_ROWS,), i32).at[pos].set(jnp.arange(TOP_K * t, dtype=i32) // TOP_K)
    return pos.reshape(t, TOP_K), src, tile_e, nsub, g_eff.astype(i32)


def _row_copy(src_hbm, dst_ref, src_row, dst_row, sem):
    return pltpu.make_async_copy(src_hbm.at[pl.ds(src_row, 1)], dst_ref.at[pl.ds(dst_row, 1)], sem)


def _gather_kernel(ns_ref, src_ref, x_hbm, o_ref, sem):
    g = pl.program_id(0)
    nsub = ns_ref[g]
    groups = nsub * (MLP_SUB // DMA_UNROLL)

    def issue(q, carry):
        for u in range(DMA_UNROLL):
            r = q * DMA_UNROLL + u
            _row_copy(x_hbm, o_ref, src_ref[0, 0, r], r, sem).start()
        return carry

    def drain(q, carry):
        for u in range(DMA_UNROLL):
            _row_copy(x_hbm, o_ref, 0, 0, sem).wait()
        return carry

    lax.fori_loop(0, groups, issue, 0)
    lax.fori_loop(0, groups, drain, 0)
    for k in range(MLP_NSUB):
        @pl.when(nsub <= k)
        def _():
            o_ref[k * MLP_SUB:(k + 1) * MLP_SUB, :] = jnp.zeros((MLP_SUB, o_ref.shape[1]), o_ref.dtype)


def _gather_rows(x, src, tile_nsub):
    n_tiles = tile_nsub.shape[0]
    d = x.shape[1]
    return pl.pallas_call(
        _gather_kernel,
        grid_spec=pltpu.PrefetchScalarGridSpec(
            num_scalar_prefetch=1, grid=(n_tiles,),
            in_specs=[pl.BlockSpec((1, 1, MLP_ROWS), lambda g, ns: (g, 0, 0), memory_space=pltpu.SMEM),
                      pl.BlockSpec(memory_space=pl.ANY)],
            out_specs=pl.BlockSpec((MLP_ROWS, d), lambda g, ns: (g, 0)),
            scratch_shapes=[pltpu.SemaphoreType.DMA(())]),
        out_shape=jax.ShapeDtypeStruct((n_tiles * MLP_ROWS, d), x.dtype),
        compiler_params=_params("arbitrary"), name="moe_gather",
    )(tile_nsub, src.reshape(n_tiles, 1, MLP_ROWS), x)


def _combine_kernel(pos_ref, pos_next_ref, h_ref, gates_ref, y_hbm, *rest, split_steps):
    if split_steps is None:
        o_ref, ya_ref, yb_ref, sem = rest
    else:
        gfin_ref, op_ref, os_ref, ya_ref, yb_ref, sem = rest
    i = pl.program_id(0)
    slot = i % 2

    def fetch(p_ref, s):
        def body(q, carry):
            for u in range(DMA_UNROLL):
                r = q * DMA_UNROLL + u
                _row_copy(y_hbm, ya_ref.at[s], p_ref[0, 0, TOP_K * r], r, sem.at[0, s]).start()
                _row_copy(y_hbm, yb_ref.at[s], p_ref[0, 0, TOP_K * r + 1], r, sem.at[1, s]).start(priority=1)
            return carry
        lax.fori_loop(0, COMBINE_ROWS // DMA_UNROLL, body, 0)

    @pl.when(i == 0)
    def _():
        fetch(pos_ref, 0)

    @pl.when(i + 1 < pl.num_programs(0))
    def _():
        fetch(pos_next_ref, 1 - slot)

    def drain(q, carry):
        for u in range(DMA_UNROLL):
            _row_copy(y_hbm, ya_ref.at[slot], 0, 0, sem.at[0, slot]).wait()
            _row_copy(y_hbm, yb_ref.at[slot], 0, 0, sem.at[1, slot]).wait()
        return carry

    lax.fori_loop(0, COMBINE_ROWS // DMA_UNROLL, drain, 0)
    gates = gates_ref[...]
    x = h_ref[...] + (gates[:, 0:1] * ya_ref[slot] + gates[:, 1:2] * yb_ref[slot])
    if split_steps is None:
        o_ref[...] = x
    else:
        y = _rms(x, gfin_ref[...])

        @pl.when(i < split_steps)
        def _():
            op_ref[...] = y

        @pl.when(i >= split_steps)
        def _():
            os_ref[...] = y


def _combine(h, gates, pos, y, *, final_norm=None, split_rows=None):
    t, d = h.shape
    rows = COMBINE_ROWS
    steps = t // rows
    pos3 = pos.reshape(steps, 1, TOP_K * rows)
    in_specs = [pl.BlockSpec((1, 1, TOP_K * rows), lambda i: (i, 0, 0), memory_space=pltpu.SMEM),
                pl.BlockSpec((1, 1, TOP_K * rows), lambda i: (jnp.minimum(i + 1, steps - 1), 0, 0),
                             memory_space=pltpu.SMEM),
                pl.BlockSpec((rows, d), lambda i: (i, 0)),
                pl.BlockSpec((rows, LANES), lambda i: (i, 0)),
                pl.BlockSpec(memory_space=pl.ANY)]
    args = [pos3, pos3, h, gates, y]
    if final_norm is None:
        split_steps = None
        out_specs = pl.BlockSpec((rows, d), lambda i: (i, 0))
        out_shape = jax.ShapeDtypeStruct((t, d), f32)
    else:
        split_steps = split_rows // rows
        in_specs.append(pl.BlockSpec((1, d), lambda i: (0, 0)))
        args.append(final_norm.reshape(1, d))
        out_specs = [pl.BlockSpec((rows, d), lambda i: (jnp.minimum(i, split_steps - 1), 0)),
                     pl.BlockSpec((rows, d), lambda i: (jnp.maximum(i - split_steps, 0), 0))]
        out_shape = [jax.ShapeDtypeStruct((split_rows, d), f32), jax.ShapeDtypeStruct((t - split_rows, d), f32)]
    return pl.pallas_call(
        functools.partial(_combine_kernel, split_steps=split_steps),
        grid=(steps,), in_specs=in_specs, out_specs=out_specs, out_shape=out_shape,
        scratch_shapes=[pltpu.VMEM((2, rows, d), f32), pltpu.VMEM((2, rows, d), f32),
                        pltpu.SemaphoreType.DMA((2, 2))],
        compiler_params=_params("arbitrary"), name="moe_combine" if final_norm is None else "moe_combine_final",
    )(*args)


def kernel(x_prompt, x_sample, state_conv, state_pool, norm_mix, norm_ffn, w_in, conv_w, pool_w, pool_scale,
           w_out, dense_w1, dense_w3, dense_w2, router_w, moe_w1, moe_w3, moe_w2, final_norm):
    n_p, len_p, d = x_prompt.shape
    n_s, len_s, _ = x_sample.shape
    t_p, t_s = n_p * len_p, n_s * len_s
    t = t_p + t_s
    depth = w_in.shape[0]
    n_moe, n_exp = moe_w1.shape[0], moe_w1.shape[1]
    x = jnp.concatenate([x_prompt.reshape(t_p, d), x_sample.reshape(t_s, d)], axis=0)
    router_pad = jnp.pad(router_w, ((0, 0), (0, 0), (0, LANES - router_w.shape[2])))
    moe_w1f = moe_w1.reshape((n_moe * n_exp,) + moe_w1.shape[2:])
    moe_w3f = moe_w3.reshape((n_moe * n_exp,) + moe_w3.shape[2:])
    moe_w2f = moe_w2.reshape((n_moe * n_exp,) + moe_w2.shape[2:])
    w_in_b, w_out_b = _to_bf16(w_in), _to_bf16(w_out)

    dense_tiles = t // MLP_ROWS
    dense_blk = jnp.arange(dense_tiles, dtype=i32)
    dense_nsub = jnp.full((dense_tiles,), MLP_NSUB, i32)
    moe_tiles = (TOP_K * t) // MLP_ROWS + n_exp

    conv_p, pool_p, conv_s, pool_s = [], [], [], []
    for l in range(depth):
        i = l // 2
        is_moe = l % 2 == 1
        outs = _block(x, l, norm_mix, w_in_b, w_out_b, state_conv, state_pool, conv_w, pool_w, pool_scale, norm_ffn,
                      router_pad if is_moe else None, i,
                      n_prompt_seq=n_p, prompt_len=len_p, n_sample_seq=n_s, sample_len=len_s)
        conv_p.append(outs[-4]); pool_p.append(outs[-3]); conv_s.append(outs[-2]); pool_s.append(outs[-1])
        if is_moe:
            h, hn, gates, idx = outs[:4]
            pos, src, tile_e, tile_nsub, tile_blk = _route_layout(idx, moe_tiles)
            xs = _gather_rows(hn, src, tile_nsub)
            ys = _mlp(xs, moe_w1f, moe_w3f, moe_w2f, tile_e + i * n_exp, tile_nsub, tile_blk)
            if l == depth - 1:
                y_prompt, y_sample = _combine(h, gates, pos, ys, final_norm=final_norm, split_rows=t_p)
            else:
                x = _combine(h, gates, pos, ys)
        else:
            h, hn = outs[:2]
            x = _mlp(hn, dense_w1, dense_w3, dense_w2, jnp.full((dense_tiles,), i, i32), dense_nsub, dense_blk, res=h)
            if l == depth - 1:
                y_prompt = _rmsnorm(x, final_norm, row_offset=0, rows=t_p)
                y_sample = _rmsnorm(x, final_norm, row_offset=t_p, rows=t_s)

    return (y_prompt.reshape(n_p, len_p, d), y_sample.reshape(n_s, len_s, d),
            jnp.stack(conv_p), jnp.stack(pool_p), jnp.stack(conv_s), jnp.stack(pool_s))
```

```python
import functools

import jax
import jax.numpy as jnp
from jax import lax
from jax.experimental import pallas as pl
from jax.experimental.pallas import tpu as pltpu

f32 = jnp.float32
bf16 = jnp.bfloat16
i32 = jnp.int32

CONV_DIM = 1024
POOL_DIM = 1024
IN_DIM = 3 * CONV_DIM + POOL_DIM
CONV_W = 3
POOL_WINDOWS = (2, 4, 8, 16)
POOL_GROUP_DIM = POOL_DIM // len(POOL_WINDOWS)
POOL_BUF = max(POOL_WINDOWS) - 1
N_EXPERTS = 8
TOP_K = 2
PAST_LEN = 16384
EPS = 1e-6

LANES = 128
HIST = 16
CONV_HIST = 8
MIX_ROWS = 256
SAMPLE_SEQS = 8
CAST_ROWS = 512
MLP_ROWS = 1024
MLP_SUB = 256
MLP_NSUB = MLP_ROWS // MLP_SUB
MLP_FC = 256
SCATTER_ROWS = 256
COMBINE_ROWS = 256
DMA_UNROLL = 8
VMEM_LIMIT = 60000 * 1024


def _params(*sem):
    return pltpu.CompilerParams(dimension_semantics=sem, vmem_limit_bytes=VMEM_LIMIT)


def _rms(x, g):
    return x * lax.rsqrt(jnp.mean(x * x, axis=-1, keepdims=True) + EPS) * g


def _norm_kernel(x_ref, g_ref, o_ref):
    o_ref[...] = _rms(x_ref[...], g_ref[...])


def _rmsnorm(x, g, *, row_offset, rows, tm=512):
    d = x.shape[1]
    off = row_offset // tm
    return pl.pallas_call(
        _norm_kernel, grid=(rows // tm,),
        in_specs=[pl.BlockSpec((tm, d), lambda i: (i + off, 0)), pl.BlockSpec((1, d), lambda i: (0, 0))],
        out_specs=pl.BlockSpec((tm, d), lambda i: (i, 0)),
        out_shape=jax.ShapeDtypeStruct((rows, d), f32),
        compiler_params=_params("arbitrary"), name="rmsnorm",
    )(x, g.reshape(1, d))


def _cast_kernel(w_ref, o_ref):
    o_ref[...] = w_ref[...].astype(o_ref.dtype)


def _to_bf16(w):
    l, k, n = w.shape
    return pl.pallas_call(
        _cast_kernel, grid=(l, k // CAST_ROWS),
        in_specs=[pl.BlockSpec((None, CAST_ROWS, n), lambda a, b: (a, b, 0))],
        out_specs=pl.BlockSpec((None, CAST_ROWS, n), lambda a, b: (a, b, 0)),
        out_shape=jax.ShapeDtypeStruct(w.shape, bf16),
        compiler_params=_params("arbitrary", "arbitrary"), name="cast_bf16",
    )(w)


def _block_kernel(x_ref, gm_ref, win_ref, wout_ref, hv_ref, hu_ref, cw_ref, pw_ref, ps_ref, gf_ref, *rest,
                  with_router, prompt_steps, steps_per_seq, sample_len):
    if with_router:
        rw_ref, h_ref, hn_ref, gates_ref, idx_ref = rest[:5]
        rest = rest[5:]
    else:
        h_ref, hn_ref = rest[:2]
        rest = rest[2:]
    ncp_ref, npp_ref, ncs_ref, nps_ref, proj_ref, ycat_ref, ev_ref, eu_ref, ev3_ref, eu3_ref = rest
    i = pl.program_id(0)
    rows = MIX_ROWS
    n_groups = len(POOL_WINDOWS)

    xn = _rms(x_ref[...], gm_ref[...]).astype(bf16)
    proj_ref[...] = jnp.dot(xn, win_ref[...], preferred_element_type=f32)

    def gated_v(r0, n):
        return proj_ref[r0:r0 + n, CONV_DIM:2 * CONV_DIM] * proj_ref[r0:r0 + n, 2 * CONV_DIM:3 * CONV_DIM]

    def conv_taps(ext):
        conv = cw_ref[0:1, :] * pltpu.roll(ext, 2, axis=0)
        conv = conv + cw_ref[1:2, :] * pltpu.roll(ext, 1, axis=0)
        return conv + cw_ref[2:3, :] * ext

    def window_sum(ext, w):
        s, k = ext, 1
        while k < w:
            s = s + pltpu.roll(s, k, axis=0)
            k *= 2
        return s

    def full_window_mean(g, s):
        w = POOL_WINDOWS[g]
        assert w & (w - 1) == 0, "scaling by 1/w equals dividing by w only for a power of two"
        return s * (1.0 / w)

    def finish(r0, n, conv, sums, mean):
        ycat_ref[r0:r0 + n, 0:CONV_DIM] = (proj_ref[r0:r0 + n, 0:CONV_DIM] * conv).astype(bf16)
        for g in range(n_groups):
            lo, hi = g * POOL_GROUP_DIM, (g + 1) * POOL_GROUP_DIM
            pooled = mean(g, sums[g]) - proj_ref[r0:r0 + n, 3 * CONV_DIM + lo:3 * CONV_DIM + hi]
            z = jnp.dot(pooled.astype(bf16), pw_ref[g].astype(bf16), preferred_element_type=f32)
            ycat_ref[r0:r0 + n, CONV_DIM + lo:CONV_DIM + hi] = (z * ps_ref[:, lo:hi]).astype(bf16)

    @pl.when(i < prompt_steps)
    def _prompt():
        t = i % steps_per_seq

        @pl.when(t == 0)
        def _():
            ev_ref[0:HIST, :] = jnp.zeros((HIST, CONV_DIM), f32)
            eu_ref[0:HIST, :] = jnp.zeros((HIST, POOL_DIM), f32)

        @pl.when(t > 0)
        def _():
            ev_ref[0:HIST, :] = ev_ref[rows:rows + HIST, :]
            eu_ref[0:HIST, :] = eu_ref[rows:rows + HIST, :]

        ev_ref[HIST:HIST + rows, :] = gated_v(0, rows)
        eu_ref[HIST:HIST + rows, :] = proj_ref[:, 3 * CONV_DIM:]
        conv = conv_taps(ev_ref[...])[HIST:, :]
        sums = [window_sum(eu_ref[:, g * POOL_GROUP_DIM:(g + 1) * POOL_GROUP_DIM], POOL_WINDOWS[g])[HIST:, :]
                for g in range(n_groups)]

        @pl.when(t == 0)
        def _():
            pos1 = lax.broadcasted_iota(i32, (rows, 1), 0) + 1
            finish(0, rows, conv, sums, lambda g, s: s / jnp.minimum(pos1, POOL_WINDOWS[g]).astype(f32))

        @pl.when(t > 0)
        def _():
            assert rows >= max(POOL_WINDOWS)
            finish(0, rows, conv, sums, full_window_mean)

        @pl.when(t == steps_per_seq - 1)
        def _():
            ncp_ref[0] = ev_ref[HIST + rows - (CONV_W - 1):HIST + rows, :]
            npp_ref[0] = eu_ref[HIST + rows - POOL_BUF:HIST + rows, :]

    @pl.when(i >= prompt_steps)
    def _sample():
        s, new = SAMPLE_SEQS, sample_len
        n = s * new
        nv, nu = s * (CONV_HIST + new), s * (HIST + new)
        assert PAST_LEN + 1 >= max(POOL_WINDOWS)
        assert new >= CONV_W - 1
        for part in range(rows // n):
            r0, s0 = part * n, part * s
            v3 = gated_v(r0, n).reshape(s, new, CONV_DIM)
            ev3_ref[:, 0:CONV_HIST - (CONV_W - 1), :] = jnp.zeros((s, CONV_HIST - (CONV_W - 1), CONV_DIM), f32)
            ev3_ref[:, CONV_HIST - (CONV_W - 1):CONV_HIST, :] = hv_ref[s0:s0 + s]
            ev3_ref[:, CONV_HIST:, :] = v3
            eu3_ref[:, 0:HIST - POOL_BUF, :] = jnp.zeros((s, HIST - POOL_BUF, POOL_DIM), f32)
            eu3_ref[:, HIST - POOL_BUF:HIST, :] = hu_ref[s0:s0 + s]
            eu3_ref[:, HIST:, :] = proj_ref[r0:r0 + n, 3 * CONV_DIM:].reshape(s, new, POOL_DIM)
            conv = conv_taps(ev3_ref[...].reshape(nv, CONV_DIM))
            conv = conv.reshape(s, CONV_HIST + new, CONV_DIM)[:, CONV_HIST:, :].reshape(n, CONV_DIM)
            sums = []
            for g in range(n_groups):
                lo, hi = g * POOL_GROUP_DIM, (g + 1) * POOL_GROUP_DIM
                sg = window_sum(eu3_ref[:, :, lo:hi].reshape(nu, POOL_GROUP_DIM), POOL_WINDOWS[g])
                sums.append(sg.reshape(s, HIST + new, POOL_GROUP_DIM)[:, HIST:, :].reshape(n, POOL_GROUP_DIM))
            finish(r0, n, conv, sums, full_window_mean)
            ncs_ref[s0:s0 + s] = v3[:, new - (CONV_W - 1):, :]
            nps_ref[s0:s0 + s] = eu3_ref[:, HIST + new - POOL_BUF:, :]

    h = x_ref[...] + jnp.dot(ycat_ref[...], wout_ref[...], preferred_element_type=f32)
    h_ref[...] = h
    hn = _rms(h, gf_ref[...])
    hn_ref[...] = hn.astype(hn_ref.dtype)
    if with_router:
        rw = rw_ref[...]
        rw_hi = rw.astype(bf16)
        rw_lo = (rw - rw_hi.astype(f32)).astype(bf16)
        hn_hi = hn.astype(bf16)
        hn_lo = (hn - hn_hi.astype(f32)).astype(bf16)
        half = rows // 2
        logits = jnp.concatenate([
            jnp.dot(hn_hi[r0:r0 + half], rw_hi, preferred_element_type=f32)
            + (jnp.dot(hn_lo[r0:r0 + half], rw_hi, preferred_element_type=f32)
               + jnp.dot(hn_hi[r0:r0 + half], rw_lo, preferred_element_type=f32))
            for r0 in (0, half)], axis=0)
        lane = lax.broadcasted_iota(i32, logits.shape, 1)
        neg = jnp.float32(-jnp.inf)
        logits = jnp.where(lane < N_EXPERTS, logits, neg)
        m1 = jnp.max(logits, axis=-1, keepdims=True)
        i1 = jnp.min(jnp.where(logits == m1, lane, LANES), axis=-1, keepdims=True)
        rest_l = jnp.where(lane == i1, neg, logits)
        m2 = jnp.max(rest_l, axis=-1, keepdims=True)
        i2 = jnp.min(jnp.where(rest_l == m2, lane, LANES), axis=-1, keepdims=True)
        e = jnp.exp(m2 - m1)
        denom = 1.0 + e
        gates_ref[...] = jnp.where(lane == 0, 1.0 / denom, jnp.where(lane == 1, e / denom, 0.0))
        idx_ref[...] = jnp.where(lane == 0, i1, jnp.where(lane == 1, i2, 0))


def _block(x, layer, norm_mix, w_in_b, w_out_b, hv, hu, conv_w, pool_w, pool_scale, norm_ffn, router_w, moe_layer,
           *, n_prompt_seq, prompt_len, n_sample_seq, sample_len):
    t_all, d = x.shape
    rows = MIX_ROWS
    steps_per_seq = prompt_len // rows
    prompt_steps = n_prompt_seq * steps_per_seq
    seqs_per_step = rows // sample_len
    sample_steps = n_sample_seq // seqs_per_step
    last_p = n_prompt_seq - 1
    n_groups = len(POOL_WINDOWS)
    with_router = router_w is not None
    once = pl.Buffered(1)

    def samp(i):
        return jnp.maximum(i - prompt_steps, 0)

    def pseq(i):
        return jnp.minimum(i // steps_per_seq, last_p)

    in_specs = [pl.BlockSpec((rows, d), lambda i: (i, 0)),
                pl.BlockSpec((None, 1, d), lambda i: (layer, 0, 0)),
                pl.BlockSpec((None, d, IN_DIM), lambda i: (layer, 0, 0), pipeline_mode=once),
                pl.BlockSpec((None, CONV_DIM + POOL_DIM, d), lambda i: (layer, 0, 0), pipeline_mode=once),
                pl.BlockSpec((None, seqs_per_step, CONV_W - 1, CONV_DIM), lambda i: (layer, samp(i), 0, 0),
                             pipeline_mode=once),
                pl.BlockSpec((None, seqs_per_step, POOL_BUF, POOL_DIM), lambda i: (layer, samp(i), 0, 0),
                             pipeline_mode=once),
                pl.BlockSpec((None, CONV_W, CONV_DIM), lambda i: (layer, 0, 0)),
                pl.BlockSpec((None, n_groups, POOL_GROUP_DIM, POOL_GROUP_DIM), lambda i: (layer, 0, 0, 0)),
                pl.BlockSpec((None, 1, POOL_DIM), lambda i: (layer, 0, 0)),
                pl.BlockSpec((None, 1, d), lambda i: (layer, 0, 0))]
    args = [x, norm_mix.reshape(-1, 1, d), w_in_b, w_out_b, hv, hu, conv_w, pool_w,
            pool_scale.reshape(-1, 1, POOL_DIM), norm_ffn.reshape(-1, 1, d)]
    out_specs = [pl.BlockSpec((rows, d), lambda i: (i, 0)), pl.BlockSpec((rows, d), lambda i: (i, 0))]
    out_shape = [jax.ShapeDtypeStruct((t_all, d), f32),
                 jax.ShapeDtypeStruct((t_all, d), f32 if with_router else bf16)]
    if with_router:
        in_specs.append(pl.BlockSpec((None, d, LANES), lambda i: (moe_layer, 0, 0)))
        args.append(router_w)
        out_specs += [pl.BlockSpec((rows, LANES), lambda i: (i, 0)), pl.BlockSpec((rows, LANES), lambda i: (i, 0))]
        out_shape += [jax.ShapeDtypeStruct((t_all, LANES), f32), jax.ShapeDtypeStruct((t_all, LANES), i32)]
    out_specs += [pl.BlockSpec((1, CONV_W - 1, CONV_DIM), lambda i: (pseq(i), 0, 0)),
                  pl.BlockSpec((1, POOL_BUF, POOL_DIM), lambda i: (pseq(i), 0, 0)),
                  pl.BlockSpec((seqs_per_step, CONV_W - 1, CONV_DIM), lambda i: (samp(i), 0, 0), pipeline_mode=once),
                  pl.BlockSpec((seqs_per_step, POOL_BUF, POOL_DIM), lambda i: (samp(i), 0, 0), pipeline_mode=once)]
    out_shape += [jax.ShapeDtypeStruct((n_prompt_seq, CONV_W - 1, CONV_DIM), f32),
                  jax.ShapeDtypeStruct((n_prompt_seq, POOL_BUF, POOL_DIM), f32),
                  jax.ShapeDtypeStruct((n_sample_seq, CONV_W - 1, CONV_DIM), f32),
                  jax.ShapeDtypeStruct((n_sample_seq, POOL_BUF, POOL_DIM), f32)]
    kern = functools.partial(_block_kernel, with_router=with_router, prompt_steps=prompt_steps,
                             steps_per_seq=steps_per_seq, sample_len=sample_len)
    return pl.pallas_call(
        kern, grid=(prompt_steps + sample_steps,), in_specs=in_specs, out_specs=out_specs, out_shape=out_shape,
        scratch_shapes=[pltpu.VMEM((rows, IN_DIM), f32), pltpu.VMEM((rows, CONV_DIM + POOL_DIM), bf16),
                        pltpu.VMEM((HIST + rows, CONV_DIM), f32), pltpu.VMEM((HIST + rows, POOL_DIM), f32),
                        pltpu.VMEM((SAMPLE_SEQS, CONV_HIST + sample_len, CONV_DIM), f32),
                        pltpu.VMEM((SAMPLE_SEQS, HIST + sample_len, POOL_DIM), f32)],
        compiler_params=_params("arbitrary"), name="block_router" if with_router else "block",
    )(*args)


def _mlp_kernel(te_ref, ns_ref, tb_ref, x_ref, w1_ref, w3_ref, w2_ref, *rest, grouped):
    del te_ref, tb_ref
    if grouped:
        o_ref, xb_ref, w1b_ref, w3b_ref, w2b_ref = rest
    else:
        res_ref, o_ref, w1b_ref, w3b_ref, w2b_ref = rest
    g, c = pl.program_id(0), pl.program_id(1)
    nsub = ns_ref[g]
    xsrc = xb_ref if grouped else x_ref

    def cast_weights():
        w1b_ref[...] = w1_ref[...].astype(bf16)
        w3b_ref[...] = w3_ref[...].astype(bf16)
        w2b_ref[...] = w2_ref[...].astype(bf16)

    def chunk(x):
        h1 = jnp.dot(x, w1b_ref[...], preferred_element_type=f32)
        h3 = jnp.dot(x, w3b_ref[...], preferred_element_type=f32)
        hid = (h1 * jax.nn.sigmoid(h1) * h3).astype(bf16)
        return jnp.dot(hid, w2b_ref[...], preferred_element_type=f32)

    @pl.when(c == 0)
    def _init():
        if grouped:
            o_ref[...] = jnp.zeros_like(o_ref)

            @pl.when(nsub > 0)
            def _():
                xb_ref[...] = x_ref[...].astype(bf16)
        else:
            o_ref[...] = res_ref[...]

    @pl.when(nsub == MLP_NSUB)
    def _full():
        cast_weights()
        o_ref[...] += chunk(xsrc[...])

    if grouped:
        @pl.when((nsub > 0) & (nsub < MLP_NSUB))
        def _partial():
            cast_weights()

            def body(k, carry):
                r = pl.multiple_of(k * MLP_SUB, MLP_SUB)
                o_ref[pl.ds(r, MLP_SUB), :] += chunk(xsrc[pl.ds(r, MLP_SUB), :])
                return carry

            lax.fori_loop(0, nsub, body, 0)


def _mlp(x, w1, w3, w2, tile_expert, tile_nsub, tile_blk, *, res=None):
    grouped = res is None
    k = x.shape[1]
    f, n = w2.shape[1], w2.shape[2]
    n_tiles = tile_expert.shape[0]
    fc = MLP_FC
    n_chunks = f // fc

    def wcol(g, c, te, ns, tb):
        return jnp.where(ns[g] > 0, c, n_chunks - 1)

    in_specs = [
        pl.BlockSpec((MLP_ROWS, k), lambda g, c, te, ns, tb: (tb[g], 0)),
        pl.BlockSpec((None, k, fc), lambda g, c, te, ns, tb: (te[g], 0, wcol(g, c, te, ns, tb))),
        pl.BlockSpec((None, k, fc), lambda g, c, te, ns, tb: (te[g], 0, wcol(g, c, te, ns, tb))),
        pl.BlockSpec((None, fc, n), lambda g, c, te, ns, tb: (te[g], wcol(g, c, te, ns, tb), 0)),
    ]
    args = [x, w1, w3, w2]
    scratch = [pltpu.VMEM((k, fc), bf16), pltpu.VMEM((k, fc), bf16), pltpu.VMEM((fc, n), bf16)]
    if grouped:
        scratch = [pltpu.VMEM((MLP_ROWS, k), bf16)] + scratch
    else:
        in_specs.append(pl.BlockSpec((MLP_ROWS, n), lambda g, c, te, ns, tb: (tb[g], 0)))
        args.append(res)
    return pl.pallas_call(
        functools.partial(_mlp_kernel, grouped=grouped),
        grid_spec=pltpu.PrefetchScalarGridSpec(
            num_scalar_prefetch=3, grid=(n_tiles, n_chunks), in_specs=in_specs,
            out_specs=pl.BlockSpec((MLP_ROWS, n), lambda g, c, te, ns, tb: (g, 0)),
            scratch_shapes=scratch),
        out_shape=jax.ShapeDtypeStruct((x.shape[0], n), f32),
        compiler_params=_params("arbitrary", "arbitrary"), name="mlp_grouped" if grouped else "mlp_dense",
    )(tile_expert, tile_nsub, tile_blk, *args)


def _route_layout(idx, n_tiles):
    t = idx.shape[0]
    flat_e = idx[:, :TOP_K].reshape(-1)
    onehot = (flat_e[:, None] == jnp.arange(N_EXPERTS, dtype=i32)[None, :]).astype(i32)
    csum = jnp.cumsum(onehot, axis=0)
    rank = jnp.sum(onehot * csum, axis=1) - 1
    cnt = csum[-1]
    tiles_e = (cnt + MLP_ROWS - 1) // MLP_ROWS
    tile_end = jnp.cumsum(tiles_e)
    tile_start = tile_end - tiles_e
    pos = jnp.sum(onehot * tile_start[None, :], axis=1) * MLP_ROWS + rank
    n_used = tile_end[-1]
    g = jnp.arange(n_tiles, dtype=i32)
    g_eff = jnp.minimum(g, n_used - 1)
    tile_e = jnp.minimum(jnp.sum((tile_end[None, :] <= g_eff[:, None]).astype(i32), axis=1), N_EXPERTS - 1)
    tile_oh = (tile_e[:, None] == jnp.arange(N_EXPERTS, dtype=i32)[None, :]).astype(i32)
    rows_in = jnp.clip(jnp.sum(tile_oh * cnt[None, :], axis=1)
                       - (g_eff - jnp.sum(tile_oh * tile_start[None, :], axis=1)) * MLP_ROWS, 0, MLP_ROWS)
    nsub = jnp.where(g < n_used, (rows_in + MLP_SUB - 1) // MLP_SUB, 0).astype(i32)
    return pos.reshape(t, TOP_K), tile_start * MLP_ROWS, cnt, tile_e, nsub, g_eff.astype(i32)


def _row_copy(src_hbm, dst_ref, src_row, dst_row, sem):
    return pltpu.make_async_copy(src_hbm.at[pl.ds(src_row, 1)], dst_ref.at[pl.ds(dst_row, 1)], sem)


def _scatter_kernel(start_ref, cnt_ref, ns_ref, pos_ref, x_hbm, o_hbm, zeros_ref, sem, zsem):
    i = pl.program_id(0)
    last = pl.num_programs(0) - 1
    slot = i % 2
    base = i * SCATTER_ROWS
    n_tiles = o_hbm.shape[0] // MLP_ROWS

    def zero_fill(act):
        for g in range(n_tiles):
            for k in range(MLP_NSUB):
                @pl.when(ns_ref[g] <= k)
                def _():
                    r0 = g * MLP_ROWS + k * MLP_SUB
                    act(pltpu.make_async_copy(zeros_ref, o_hbm.at[pl.ds(r0, MLP_SUB)], zsem))
        for e in range(N_EXPERTS):
            tail = (-cnt_ref[e]) & (MLP_SUB - 1)
            first = start_ref[e] + cnt_ref[e]

            def one_row(j, carry):
                act(_row_copy(zeros_ref, o_hbm, 0, first + j, zsem))
                return carry

            lax.fori_loop(0, tail, one_row, 0)

    @pl.when(i == 0)
    def _():
        zeros_ref[...] = jnp.zeros_like(zeros_ref)
        zero_fill(lambda cp: cp.start())

    def issue(q, carry):
        r0 = pl.multiple_of(q * DMA_UNROLL, DMA_UNROLL)
        row0 = pl.multiple_of(base + r0, DMA_UNROLL)
        for u in range(DMA_UNROLL):
            for k in range(TOP_K):
                dst = pos_ref[0, 0, TOP_K * (r0 + u) + k]
                _row_copy(x_hbm, o_hbm, row0 + u, dst, sem.at[slot]).start(priority=k)
        return carry

    def drain(s):
        def body(q, carry):
            for u in range(DMA_UNROLL * TOP_K):
                _row_copy(x_hbm, o_hbm, 0, 0, sem.at[s]).wait()
            return carry
        lax.fori_loop(0, SCATTER_ROWS // DMA_UNROLL, body, 0)

    lax.fori_loop(0, SCATTER_ROWS // DMA_UNROLL, issue, 0)

    @pl.when(i > 0)
    def _():
        drain(1 - slot)

    @pl.when(i == last)
    def _():
        drain(slot)
        zero_fill(lambda cp: cp.wait())


def _scatter_rows(x, pos, row_start, cnt, tile_nsub):
    t, d = x.shape
    n_tiles = tile_nsub.shape[0]
    steps = t // SCATTER_ROWS
    return pl.pallas_call(
        _scatter_kernel,
        grid_spec=pltpu.PrefetchScalarGridSpec(
            num_scalar_prefetch=3, grid=(steps,),
            in_specs=[pl.BlockSpec((1, 1, TOP_K * SCATTER_ROWS), lambda i, *_: (i, 0, 0), memory_space=pltpu.SMEM),
                      pl.BlockSpec(memory_space=pl.ANY)],
            out_specs=pl.BlockSpec(memory_space=pl.ANY),
            scratch_shapes=[pltpu.VMEM((MLP_SUB, d), x.dtype), pltpu.SemaphoreType.DMA((2,)),
                            pltpu.SemaphoreType.DMA(())]),
        out_shape=jax.ShapeDtypeStruct((n_tiles * MLP_ROWS, d), x.dtype),
        compiler_params=_params("arbitrary"), name="moe_scatter",
    )(row_start, cnt, tile_nsub, pos.reshape(steps, 1, TOP_K * SCATTER_ROWS), x)


def _combine_kernel(pos_ref, pos_next_ref, h_ref, gates_ref, y_hbm, *rest, split_steps):
    if split_steps is None:
        o_ref, ya_ref, yb_ref, sem = rest
    else:
        gfin_ref, op_ref, os_ref, ya_ref, yb_ref, sem = rest
    i = pl.program_id(0)
    slot = i % 2

    def fetch(p_ref, s):
        def body(q, carry):
            for u in range(DMA_UNROLL):
                r = q * DMA_UNROLL + u
                _row_copy(y_hbm, ya_ref.at[s], p_ref[0, 0, TOP_K * r], r, sem.at[0, s]).start()
                _row_copy(y_hbm, yb_ref.at[s], p_ref[0, 0, TOP_K * r + 1], r, sem.at[1, s]).start(priority=1)
            return carry
        lax.fori_loop(0, COMBINE_ROWS // DMA_UNROLL, body, 0)

    @pl.when(i == 0)
    def _():
        fetch(pos_ref, 0)

    @pl.when(i + 1 < pl.num_programs(0))
    def _():
        fetch(pos_next_ref, 1 - slot)

    def drain(q, carry):
        for u in range(DMA_UNROLL):
            _row_copy(y_hbm, ya_ref.at[slot], 0, 0, sem.at[0, slot]).wait()
            _row_copy(y_hbm, yb_ref.at[slot], 0, 0, sem.at[1, slot]).wait()
        return carry

    lax.fori_loop(0, COMBINE_ROWS // DMA_UNROLL, drain, 0)
    gates = gates_ref[...]
    x = h_ref[...] + (gates[:, 0:1] * ya_ref[slot] + gates[:, 1:2] * yb_ref[slot])
    if split_steps is None:
        o_ref[...] = x
    else:
        y = _rms(x, gfin_ref[...])

        @pl.when(i < split_steps)
        def _():
            op_ref[...] = y

        @pl.when(i >= split_steps)
        def _():
            os_ref[...] = y


def _combine(h, gates, pos, y, *, final_norm=None, split_rows=None):
    t, d = h.shape
    rows = COMBINE_ROWS
    steps = t // rows
    pos3 = pos.reshape(steps, 1, TOP_K * rows)
    in_specs = [pl.BlockSpec((1, 1, TOP_K * rows), lambda i: (i, 0, 0), memory_space=pltpu.SMEM),
                pl.BlockSpec((1, 1, TOP_K * rows), lambda i: (jnp.minimum(i + 1, steps - 1), 0, 0),
                             memory_space=pltpu.SMEM),
                pl.BlockSpec((rows, d), lambda i: (i, 0)),
                pl.BlockSpec((rows, LANES), lambda i: (i, 0)),
                pl.BlockSpec(memory_space=pl.ANY)]
    args = [pos3, pos3, h, gates, y]
    if final_norm is None:
        split_steps = None
        out_specs = pl.BlockSpec((rows, d), lambda i: (i, 0))
        out_shape = jax.ShapeDtypeStruct((t, d), f32)
    else:
        split_steps = split_rows // rows
        in_specs.append(pl.BlockSpec((1, d), lambda i: (0, 0)))
        args.append(final_norm.reshape(1, d))
        out_specs = [pl.BlockSpec((rows, d), lambda i: (jnp.minimum(i, split_steps - 1), 0)),
                     pl.BlockSpec((rows, d), lambda i: (jnp.maximum(i - split_steps, 0), 0))]
        out_shape = [jax.ShapeDtypeStruct((split_rows, d), f32), jax.ShapeDtypeStruct((t - split_rows, d), f32)]
    return pl.pallas_call(
        functools.partial(_combine_kernel, split_steps=split_steps),
        grid=(steps,), in_specs=in_specs, out_specs=out_specs, out_shape=out_shape,
        scratch_shapes=[pltpu.VMEM((2, rows, d), f32), pltpu.VMEM((2, rows, d), f32),
                        pltpu.SemaphoreType.DMA((2, 2))],
        compiler_params=_params("arbitrary"), name="moe_combine" if final_norm is None else "moe_combine_final",
    )(*args)


def kernel(x_prompt, x_sample, state_conv, state_pool, norm_mix, norm_ffn, w_in, conv_w, pool_w, pool_scale,
           w_out, dense_w1, dense_w3, dense_w2, router_w, moe_w1, moe_w3, moe_w2, final_norm):
    n_p, len_p, d = x_prompt.shape
    n_s, len_s, _ = x_sample.shape
    t_p, t_s = n_p * len_p, n_s * len_s
    t = t_p + t_s
    depth = w_in.shape[0]
    n_moe, n_exp = moe_w1.shape[0], moe_w1.shape[1]
    x = jnp.concatenate([x_prompt.reshape(t_p, d), x_sample.reshape(t_s, d)], axis=0)
    router_pad = jnp.pad(router_w, ((0, 0), (0, 0), (0, LANES - router_w.shape[2])))
    moe_w1f = moe_w1.reshape((n_moe * n_exp,) + moe_w1.shape[2:])
    moe_w3f = moe_w3.reshape((n_moe * n_exp,) + moe_w3.shape[2:])
    moe_w2f = moe_w2.reshape((n_moe * n_exp,) + moe_w2.shape[2:])
    w_in_b, w_out_b = _to_bf16(w_in), _to_bf16(w_out)

    dense_tiles = t // MLP_ROWS
    dense_blk = jnp.arange(dense_tiles, dtype=i32)
    dense_nsub = jnp.full((dense_tiles,), MLP_NSUB, i32)
    moe_tiles = (TOP_K * t) // MLP_ROWS + n_exp

    conv_p, pool_p, conv_s, pool_s = [], [], [], []
    for l in range(depth):
        i = l // 2
        is_moe = l % 2 == 1
        outs = _block(x, l, norm_mix, w_in_b, w_out_b, state_conv, state_pool, conv_w, pool_w, pool_scale, norm_ffn,
                      router_pad if is_moe else None, i,
                      n_prompt_seq=n_p, prompt_len=len_p, n_sample_seq=n_s, sample_len=len_s)
        conv_p.append(outs[-4]); pool_p.append(outs[-3]); conv_s.append(outs[-2]); pool_s.append(outs[-1])
        if is_moe:
            h, hn, gates, idx = outs[:4]
            pos, row_start, cnt, tile_e, tile_nsub, tile_blk = _route_layout(idx, moe_tiles)
            xs = _scatter_rows(hn, pos, row_start, cnt, tile_nsub)
            ys = _mlp(xs, moe_w1f, moe_w3f, moe_w2f, tile_e + i * n_exp, tile_nsub, tile_blk)
            if l == depth - 1:
                y_prompt, y_sample = _combine(h, gates, pos, ys, final_norm=final_norm, split_rows=t_p)
            else:
                x = _combine(h, gates, pos, ys)
        else:
            h, hn = outs[:2]
            x = _mlp(hn, dense_w1, dense_w3, dense_w2, jnp.full((dense_tiles,), i, i32), dense_nsub, dense_blk, res=h)
            if l == depth - 1:
                y_prompt = _rmsnorm(x, final_norm, row_offset=0, rows=t_p)
                y_sample = _rmsnorm(x, final_norm, row_offset=t_p, rows=t_s)

    return (y_prompt.reshape(n_p, len_p, d), y_sample.reshape(n_s, len_s, d),
            jnp.stack(conv_p), jnp.stack(pool_p), jnp.stack(conv_s), jnp.stack(pool_s))
```

```python
import functools

import jax
import jax.numpy as jnp
from jax import lax
from jax.experimental import pallas as pl
from jax.experimental.pallas import tpu as pltpu

f32 = jnp.float32
bf16 = jnp.bfloat16
i32 = jnp.int32

CONV_DIM = 1024
POOL_DIM = 1024
IN_DIM = 3 * CONV_DIM + POOL_DIM
CONV_W = 3
POOL_WINDOWS = (2, 4, 8, 16)
POOL_GROUP_DIM = POOL_DIM // len(POOL_WINDOWS)
POOL_BUF = max(POOL_WINDOWS) - 1
N_EXPERTS = 8
TOP_K = 2
PAST_LEN = 16384
EPS = 1e-6

LANES = 128
HIST = 16
CONV_HIST = 8
MIX_ROWS = 256
SAMPLE_SEQS = 8
CAST_ROWS = 512
MLP_ROWS = 1024
MLP_SUB = 256
MLP_NSUB = MLP_ROWS // MLP_SUB
MLP_FC = 256
SCATTER_ROWS = 256
COMBINE_ROWS = 256
DMA_UNROLL = 8
VMEM_LIMIT = 60000 * 1024


def _params(*sem):
    return pltpu.CompilerParams(dimension_semantics=sem, vmem_limit_bytes=VMEM_LIMIT)


def _rms(x, g):
    return x * lax.rsqrt(jnp.mean(x * x, axis=-1, keepdims=True) + EPS) * g


def _norm_kernel(x_ref, g_ref, o_ref):
    o_ref[...] = _rms(x_ref[...], g_ref[...])


def _rmsnorm(x, g, *, row_offset, rows, tm=512):
    d = x.shape[1]
    off = row_offset // tm
    return pl.pallas_call(
        _norm_kernel, grid=(rows // tm,),
        in_specs=[pl.BlockSpec((tm, d), lambda i: (i + off, 0)), pl.BlockSpec((1, d), lambda i: (0, 0))],
        out_specs=pl.BlockSpec((tm, d), lambda i: (i, 0)),
        out_shape=jax.ShapeDtypeStruct((rows, d), f32),
        compiler_params=_params("arbitrary"), name="rmsnorm",
    )(x, g.reshape(1, d))


def _cast_kernel(w_ref, o_ref):
    o_ref[...] = w_ref[...].astype(o_ref.dtype)


def _to_bf16(w):
    l, k, n = w.shape
    return pl.pallas_call(
        _cast_kernel, grid=(l, k // CAST_ROWS),
        in_specs=[pl.BlockSpec((None, CAST_ROWS, n), lambda a, b: (a, b, 0))],
        out_specs=pl.BlockSpec((None, CAST_ROWS, n), lambda a, b: (a, b, 0)),
        out_shape=jax.ShapeDtypeStruct(w.shape, bf16),
        compiler_params=_params("arbitrary", "arbitrary"), name="cast_bf16",
    )(w)


def _block_kernel(*refs, with_router, split_x, prompt_steps, steps_per_seq, sample_len):
    if split_x:
        xp_ref, xs_ref = refs[:2]
        refs = refs[2:]

        def read_x():
            return jnp.where(pl.program_id(0) < prompt_steps, xp_ref[...], xs_ref[...])
    else:
        x_ref = refs[0]
        refs = refs[1:]

        def read_x():
            return x_ref[...]
    gm_ref, win_ref, wout_ref, hv_ref, hu_ref, cw_ref, pw_ref, ps_ref, gf_ref = refs[:9]
    rest = refs[9:]
    if with_router:
        rw_ref, h_ref, hn_ref, gates_ref, idx_ref = rest[:5]
        rest = rest[5:]
    else:
        h_ref, hn_ref = rest[:2]
        rest = rest[2:]
    ncp_ref, npp_ref, ncs_ref, nps_ref, proj_ref, ycat_ref, ev_ref, eu_ref, ev3_ref, eu3_ref = rest
    i = pl.program_id(0)
    rows = MIX_ROWS
    n_groups = len(POOL_WINDOWS)

    xn = _rms(read_x(), gm_ref[...]).astype(bf16)
    proj_ref[...] = jnp.dot(xn, win_ref[...], preferred_element_type=f32)

    def gated_v(r0, n):
        return proj_ref[r0:r0 + n, CONV_DIM:2 * CONV_DIM] * proj_ref[r0:r0 + n, 2 * CONV_DIM:3 * CONV_DIM]

    def conv_taps(ext):
        conv = cw_ref[0:1, :] * pltpu.roll(ext, 2, axis=0)
        conv = conv + cw_ref[1:2, :] * pltpu.roll(ext, 1, axis=0)
        return conv + cw_ref[2:3, :] * ext

    def window_sum(ext, w):
        s, k = ext, 1
        while k < w:
            s = s + pltpu.roll(s, k, axis=0)
            k *= 2
        return s

    def full_window_mean(g, s):
        w = POOL_WINDOWS[g]
        assert w & (w - 1) == 0, "scaling by 1/w equals dividing by w only for a power of two"
        return s * (1.0 / w)

    def finish(r0, n, conv, sums, mean):
        ycat_ref[r0:r0 + n, 0:CONV_DIM] = (proj_ref[r0:r0 + n, 0:CONV_DIM] * conv).astype(bf16)
        for g in range(n_groups):
            lo, hi = g * POOL_GROUP_DIM, (g + 1) * POOL_GROUP_DIM
            pooled = mean(g, sums[g]) - proj_ref[r0:r0 + n, 3 * CONV_DIM + lo:3 * CONV_DIM + hi]
            z = jnp.dot(pooled.astype(bf16), pw_ref[g].astype(bf16), preferred_element_type=f32)
            ycat_ref[r0:r0 + n, CONV_DIM + lo:CONV_DIM + hi] = (z * ps_ref[:, lo:hi]).astype(bf16)

    @pl.when(i < prompt_steps)
    def _prompt():
        t = i % steps_per_seq

        @pl.when(t == 0)
        def _():
            ev_ref[0:HIST, :] = jnp.zeros((HIST, CONV_DIM), f32)
            eu_ref[0:HIST, :] = jnp.zeros((HIST, POOL_DIM), f32)

        @pl.when(t > 0)
        def _():
            ev_ref[0:HIST, :] = ev_ref[rows:rows + HIST, :]
            eu_ref[0:HIST, :] = eu_ref[rows:rows + HIST, :]

        ev_ref[HIST:HIST + rows, :] = gated_v(0, rows)
        eu_ref[HIST:HIST + rows, :] = proj_ref[:, 3 * CONV_DIM:]
        conv = conv_taps(ev_ref[...])[HIST:, :]
        sums = [window_sum(eu_ref[:, g * POOL_GROUP_DIM:(g + 1) * POOL_GROUP_DIM], POOL_WINDOWS[g])[HIST:, :]
                for g in range(n_groups)]

        @pl.when(t == 0)
        def _():
            pos1 = lax.broadcasted_iota(i32, (rows, 1), 0) + 1
            finish(0, rows, conv, sums, lambda g, s: s / jnp.minimum(pos1, POOL_WINDOWS[g]).astype(f32))

        @pl.when(t > 0)
        def _():
            assert rows >= max(POOL_WINDOWS)
            finish(0, rows, conv, sums, full_window_mean)

        @pl.when(t == steps_per_seq - 1)
        def _():
            ncp_ref[0] = ev_ref[HIST + rows - (CONV_W - 1):HIST + rows, :]
            npp_ref[0] = eu_ref[HIST + rows - POOL_BUF:HIST + rows, :]

    @pl.when(i >= prompt_steps)
    def _sample():
        s, new = SAMPLE_SEQS, sample_len
        n = s * new
        nv, nu = s * (CONV_HIST + new), s * (HIST + new)
        assert PAST_LEN + 1 >= max(POOL_WINDOWS)
        assert new >= CONV_W - 1
        for part in range(rows // n):
            r0, s0 = part * n, part * s
            v3 = gated_v(r0, n).reshape(s, new, CONV_DIM)
            ev3_ref[:, 0:CONV_HIST - (CONV_W - 1), :] = jnp.zeros((s, CONV_HIST - (CONV_W - 1), CONV_DIM), f32)
            ev3_ref[:, CONV_HIST - (CONV_W - 1):CONV_HIST, :] = hv_ref[s0:s0 + s]
            ev3_ref[:, CONV_HIST:, :] = v3
            eu3_ref[:, 0:HIST - POOL_BUF, :] = jnp.zeros((s, HIST - POOL_BUF, POOL_DIM), f32)
            eu3_ref[:, HIST - POOL_BUF:HIST, :] = hu_ref[s0:s0 + s]
            eu3_ref[:, HIST:, :] = proj_ref[r0:r0 + n, 3 * CONV_DIM:].reshape(s, new, POOL_DIM)
            conv = conv_taps(ev3_ref[...].reshape(nv, CONV_DIM))
            conv = conv.reshape(s, CONV_HIST + new, CONV_DIM)[:, CONV_HIST:, :].reshape(n, CONV_DIM)
            sums = []
            for g in range(n_groups):
                lo, hi = g * POOL_GROUP_DIM, (g + 1) * POOL_GROUP_DIM
                sg = window_sum(eu3_ref[:, :, lo:hi].reshape(nu, POOL_GROUP_DIM), POOL_WINDOWS[g])
                sums.append(sg.reshape(s, HIST + new, POOL_GROUP_DIM)[:, HIST:, :].reshape(n, POOL_GROUP_DIM))
            finish(r0, n, conv, sums, full_window_mean)
            ncs_ref[s0:s0 + s] = v3[:, new - (CONV_W - 1):, :]
            nps_ref[s0:s0 + s] = eu3_ref[:, HIST + new - POOL_BUF:, :]

    h = read_x() + jnp.dot(ycat_ref[...], wout_ref[...], preferred_element_type=f32)
    h_ref[...] = h
    hn = _rms(h, gf_ref[...])
    hn_ref[...] = hn.astype(hn_ref.dtype)
    if with_router:
        rw = rw_ref[...]
        rw_hi = rw.astype(bf16)
        rw_lo = (rw - rw_hi.astype(f32)).astype(bf16)
        hn_hi = hn.astype(bf16)
        hn_lo = (hn - hn_hi.astype(f32)).astype(bf16)
        half = rows // 2
        logits = jnp.concatenate([
            jnp.dot(hn_hi[r0:r0 + half], rw_hi, preferred_element_type=f32)
            + (jnp.dot(hn_lo[r0:r0 + half], rw_hi, preferred_element_type=f32)
               + jnp.dot(hn_hi[r0:r0 + half], rw_lo, preferred_element_type=f32))
            for r0 in (0, half)], axis=0)
        lane = lax.broadcasted_iota(i32, logits.shape, 1)
        neg = jnp.float32(-jnp.inf)
        logits = jnp.where(lane < N_EXPERTS, logits, neg)
        m1 = jnp.max(logits, axis=-1, keepdims=True)
        i1 = jnp.min(jnp.where(logits == m1, lane, LANES), axis=-1, keepdims=True)
        rest_l = jnp.where(lane == i1, neg, logits)
        m2 = jnp.max(rest_l, axis=-1, keepdims=True)
        i2 = jnp.min(jnp.where(rest_l == m2, lane, LANES), axis=-1, keepdims=True)
        e = jnp.exp(m2 - m1)
        denom = 1.0 + e
        gates_ref[...] = jnp.where(lane == 0, 1.0 / denom, jnp.where(lane == 1, e / denom, 0.0))
        idx_ref[...] = jnp.where(lane == 0, i1, jnp.where(lane == 1, i2, 0))


def _block(x, layer, norm_mix, w_in_b, w_out_b, hv, hu, conv_w, pool_w, pool_scale, norm_ffn, router_w, moe_layer,
           *, n_prompt_seq, prompt_len, n_sample_seq, sample_len):
    split_x = isinstance(x, tuple)
    d = (x[0] if split_x else x).shape[1]
    t_all = sum(a.shape[0] for a in x) if split_x else x.shape[0]
    rows = MIX_ROWS
    steps_per_seq = prompt_len // rows
    prompt_steps = n_prompt_seq * steps_per_seq
    seqs_per_step = rows // sample_len
    sample_steps = n_sample_seq // seqs_per_step
    last_p = n_prompt_seq - 1
    n_groups = len(POOL_WINDOWS)
    with_router = router_w is not None
    once = pl.Buffered(1)

    def samp(i):
        return jnp.maximum(i - prompt_steps, 0)

    def pseq(i):
        return jnp.minimum(i // steps_per_seq, last_p)

    if split_x:
        x_specs = [pl.BlockSpec((rows, d), lambda i: (jnp.minimum(i, prompt_steps - 1), 0)),
                   pl.BlockSpec((rows, d), lambda i: (samp(i), 0))]
        x_args = list(x)
    else:
        x_specs = [pl.BlockSpec((rows, d), lambda i: (i, 0))]
        x_args = [x]
    in_specs = x_specs + [
                pl.BlockSpec((None, 1, d), lambda i: (layer, 0, 0)),
                pl.BlockSpec((None, d, IN_DIM), lambda i: (layer, 0, 0), pipeline_mode=once),
                pl.BlockSpec((None, CONV_DIM + POOL_DIM, d), lambda i: (layer, 0, 0), pipeline_mode=once),
                pl.BlockSpec((None, seqs_per_step, CONV_W - 1, CONV_DIM), lambda i: (layer, samp(i), 0, 0),
                             pipeline_mode=once),
                pl.BlockSpec((None, seqs_per_step, POOL_BUF, POOL_DIM), lambda i: (layer, samp(i), 0, 0),
                             pipeline_mode=once),
                pl.BlockSpec((None, CONV_W, CONV_DIM), lambda i: (layer, 0, 0)),
                pl.BlockSpec((None, n_groups, POOL_GROUP_DIM, POOL_GROUP_DIM), lambda i: (layer, 0, 0, 0)),
                pl.BlockSpec((None, 1, POOL_DIM), lambda i: (layer, 0, 0)),
                pl.BlockSpec((None, 1, d), lambda i: (layer, 0, 0))]
    args = x_args + [norm_mix.reshape(-1, 1, d), w_in_b, w_out_b, hv, hu, conv_w, pool_w,
                     pool_scale.reshape(-1, 1, POOL_DIM), norm_ffn.reshape(-1, 1, d)]
    out_specs = [pl.BlockSpec((rows, d), lambda i: (i, 0)), pl.BlockSpec((rows, d), lambda i: (i, 0))]
    out_shape = [jax.ShapeDtypeStruct((t_all, d), f32),
                 jax.ShapeDtypeStruct((t_all, d), f32 if with_router else bf16)]
    if with_router:
        in_specs.append(pl.BlockSpec((None, d, LANES), lambda i: (moe_layer, 0, 0)))
        args.append(router_w)
        out_specs += [pl.BlockSpec((rows, LANES), lambda i: (i, 0)), pl.BlockSpec((rows, LANES), lambda i: (i, 0))]
        out_shape += [jax.ShapeDtypeStruct((t_all, LANES), f32), jax.ShapeDtypeStruct((t_all, LANES), i32)]
    out_specs += [pl.BlockSpec((1, CONV_W - 1, CONV_DIM), lambda i: (pseq(i), 0, 0)),
                  pl.BlockSpec((1, POOL_BUF, POOL_DIM), lambda i: (pseq(i), 0, 0)),
                  pl.BlockSpec((seqs_per_step, CONV_W - 1, CONV_DIM), lambda i: (samp(i), 0, 0), pipeline_mode=once),
                  pl.BlockSpec((seqs_per_step, POOL_BUF, POOL_DIM), lambda i: (samp(i), 0, 0), pipeline_mode=once)]
    out_shape += [jax.ShapeDtypeStruct((n_prompt_seq, CONV_W - 1, CONV_DIM), f32),
                  jax.ShapeDtypeStruct((n_prompt_seq, POOL_BUF, POOL_DIM), f32),
                  jax.ShapeDtypeStruct((n_sample_seq, CONV_W - 1, CONV_DIM), f32),
                  jax.ShapeDtypeStruct((n_sample_seq, POOL_BUF, POOL_DIM), f32)]
    kern = functools.partial(_block_kernel, with_router=with_router, split_x=split_x, prompt_steps=prompt_steps,
                             steps_per_seq=steps_per_seq, sample_len=sample_len)
    return pl.pallas_call(
        kern, grid=(prompt_steps + sample_steps,), in_specs=in_specs, out_specs=out_specs, out_shape=out_shape,
        scratch_shapes=[pltpu.VMEM((rows, IN_DIM), f32), pltpu.VMEM((rows, CONV_DIM + POOL_DIM), bf16),
                        pltpu.VMEM((HIST + rows, CONV_DIM), f32), pltpu.VMEM((HIST + rows, POOL_DIM), f32),
                        pltpu.VMEM((SAMPLE_SEQS, CONV_HIST + sample_len, CONV_DIM), f32),
                        pltpu.VMEM((SAMPLE_SEQS, HIST + sample_len, POOL_DIM), f32)],
        compiler_params=_params("arbitrary"), name="block_router" if with_router else "block",
    )(*args)


def _mlp_kernel(te_ref, ns_ref, tb_ref, x_ref, w1_ref, w3_ref, w2_ref, *rest, grouped):
    del te_ref, tb_ref
    if grouped:
        o_ref, xb_ref, w1b_ref, w3b_ref, w2b_ref = rest
    else:
        res_ref, o_ref, w1b_ref, w3b_ref, w2b_ref = rest
    g, c = pl.program_id(0), pl.program_id(1)
    nsub = ns_ref[g]
    xsrc = xb_ref if grouped else x_ref

    def cast_weights():
        w1b_ref[...] = w1_ref[...].astype(bf16)
        w3b_ref[...] = w3_ref[...].astype(bf16)
        w2b_ref[...] = w2_ref[...].astype(bf16)

    def chunk(x):
        h1 = jnp.dot(x, w1b_ref[...], preferred_element_type=f32)
        h3 = jnp.dot(x, w3b_ref[...], preferred_element_type=f32)
        hid = (h1 * jax.nn.sigmoid(h1) * h3).astype(bf16)
        return jnp.dot(hid, w2b_ref[...], preferred_element_type=f32)

    @pl.when(c == 0)
    def _init():
        if grouped:
            o_ref[...] = jnp.zeros_like(o_ref)

            @pl.when(nsub > 0)
            def _():
                xb_ref[...] = x_ref[...].astype(bf16)
        else:
            o_ref[...] = res_ref[...]

    for k in range(1 if grouped else MLP_NSUB, MLP_NSUB + 1):
        @pl.when(nsub == k)
        def _():
            m = k * MLP_SUB
            cast_weights()
            o_ref[0:m, :] += chunk(xsrc[0:m, :])


def _mlp(x, w1, w3, w2, tile_expert, tile_nsub, tile_blk, *, res=None):
    grouped = res is None
    k = x.shape[1]
    f, n = w2.shape[1], w2.shape[2]
    n_tiles = tile_expert.shape[0]
    fc = MLP_FC
    n_chunks = f // fc

    def wcol(g, c, te, ns, tb):
        return jnp.where(ns[g] > 0, c, n_chunks - 1)

    in_specs = [
        pl.BlockSpec((MLP_ROWS, k), lambda g, c, te, ns, tb: (tb[g], 0)),
        pl.BlockSpec((None, k, fc), lambda g, c, te, ns, tb: (te[g], 0, wcol(g, c, te, ns, tb))),
        pl.BlockSpec((None, k, fc), lambda g, c, te, ns, tb: (te[g], 0, wcol(g, c, te, ns, tb))),
        pl.BlockSpec((None, fc, n), lambda g, c, te, ns, tb: (te[g], wcol(g, c, te, ns, tb), 0)),
    ]
    args = [x, w1, w3, w2]
    scratch = [pltpu.VMEM((k, fc), bf16), pltpu.VMEM((k, fc), bf16), pltpu.VMEM((fc, n), bf16)]
    if grouped:
        scratch = [pltpu.VMEM((MLP_ROWS, k), bf16)] + scratch
    else:
        in_specs.append(pl.BlockSpec((MLP_ROWS, n), lambda g, c, te, ns, tb: (tb[g], 0)))
        args.append(res)
    return pl.pallas_call(
        functools.partial(_mlp_kernel, grouped=grouped),
        grid_spec=pltpu.PrefetchScalarGridSpec(
            num_scalar_prefetch=3, grid=(n_tiles, n_chunks), in_specs=in_specs,
            out_specs=pl.BlockSpec((MLP_ROWS, n), lambda g, c, te, ns, tb: (g, 0)),
            scratch_shapes=scratch),
        out_shape=jax.ShapeDtypeStruct((x.shape[0], n), f32),
        compiler_params=_params("arbitrary", "arbitrary"), name="mlp_grouped" if grouped else "mlp_dense",
    )(tile_expert, tile_nsub, tile_blk, *args)


def _route_layout(idx, n_tiles):
    t = idx.shape[0]
    flat_e = idx[:, :TOP_K].reshape(-1)
    onehot = (flat_e[:, None] == jnp.arange(N_EXPERTS, dtype=i32)[None, :]).astype(i32)
    csum = jnp.cumsum(onehot, axis=0)
    rank = jnp.sum(onehot * csum, axis=1) - 1
    cnt = csum[-1]
    tiles_e = (cnt + MLP_ROWS - 1) // MLP_ROWS
    tile_end = jnp.cumsum(tiles_e)
    tile_start = tile_end - tiles_e
    pos = jnp.sum(onehot * tile_start[None, :], axis=1) * MLP_ROWS + rank
    n_used = tile_end[-1]
    g = jnp.arange(n_tiles, dtype=i32)
    g_eff = jnp.minimum(g, n_used - 1)
    tile_e = jnp.minimum(jnp.sum((tile_end[None, :] <= g_eff[:, None]).astype(i32), axis=1), N_EXPERTS - 1)
    tile_oh = (tile_e[:, None] == jnp.arange(N_EXPERTS, dtype=i32)[None, :]).astype(i32)
    rows_in = jnp.clip(jnp.sum(tile_oh * cnt[None, :], axis=1)
                       - (g_eff - jnp.sum(tile_oh * tile_start[None, :], axis=1)) * MLP_ROWS, 0, MLP_ROWS)
    nsub = jnp.where(g < n_used, (rows_in + MLP_SUB - 1) // MLP_SUB, 0).astype(i32)
    return pos.reshape(t, TOP_K), tile_start * MLP_ROWS, cnt, tile_e, nsub, g_eff.astype(i32)


def _row_copy(src_hbm, dst_ref, src_row, dst_row, sem):
    return pltpu.make_async_copy(src_hbm.at[pl.ds(src_row, 1)], dst_ref.at[pl.ds(dst_row, 1)], sem)


def _scatter_kernel(start_ref, cnt_ref, ns_ref, pos_ref, x_ref, o_hbm, zeros_ref, sem, zsem):
    i = pl.program_id(0)
    last = pl.num_programs(0) - 1
    n_tiles = o_hbm.shape[0] // MLP_ROWS

    def zero_fill(act):
        for g in range(n_tiles):
            for k in range(MLP_NSUB):
                @pl.when(ns_ref[g] <= k)
                def _():
                    r0 = g * MLP_ROWS + k * MLP_SUB
                    act(pltpu.make_async_copy(zeros_ref, o_hbm.at[pl.ds(r0, MLP_SUB)], zsem))
        for e in range(N_EXPERTS):
            tail = (-cnt_ref[e]) & (MLP_SUB - 1)
            first = start_ref[e] + cnt_ref[e]

            def one_row(j, carry):
                act(_row_copy(zeros_ref, o_hbm, 0, first + j, zsem))
                return carry

            lax.fori_loop(0, tail, one_row, 0)

    @pl.when(i == 0)
    def _():
        zeros_ref[...] = jnp.zeros_like(zeros_ref)
        zero_fill(lambda cp: cp.start())

    def issue(q, carry):
        r0 = pl.multiple_of(q * DMA_UNROLL, DMA_UNROLL)
        for u in range(DMA_UNROLL):
            for k in range(TOP_K):
                _row_copy(x_ref, o_hbm, r0 + u, pos_ref[0, 0, TOP_K * (r0 + u) + k], sem.at[k]).start(priority=k)
        return carry

    def drain(q, carry):
        for u in range(DMA_UNROLL):
            for k in range(TOP_K):
                _row_copy(x_ref, o_hbm, 0, 0, sem.at[k]).wait()
        return carry

    lax.fori_loop(0, SCATTER_ROWS // DMA_UNROLL, issue, 0)
    lax.fori_loop(0, SCATTER_ROWS // DMA_UNROLL, drain, 0)

    @pl.when(i == last)
    def _():
        zero_fill(lambda cp: cp.wait())


def _scatter_rows(x, pos, row_start, cnt, tile_nsub):
    t, d = x.shape
    n_tiles = tile_nsub.shape[0]
    steps = t // SCATTER_ROWS
    return pl.pallas_call(
        _scatter_kernel,
        grid_spec=pltpu.PrefetchScalarGridSpec(
            num_scalar_prefetch=3, grid=(steps,),
            in_specs=[pl.BlockSpec((1, 1, TOP_K * SCATTER_ROWS), lambda i, *_: (i, 0, 0), memory_space=pltpu.SMEM),
                      pl.BlockSpec((SCATTER_ROWS, d), lambda i, *_: (i, 0))],
            out_specs=pl.BlockSpec(memory_space=pl.ANY),
            scratch_shapes=[pltpu.VMEM((MLP_SUB, d), x.dtype), pltpu.SemaphoreType.DMA((TOP_K,)),
                            pltpu.SemaphoreType.DMA(())]),
        out_shape=jax.ShapeDtypeStruct((n_tiles * MLP_ROWS, d), x.dtype),
        compiler_params=_params("arbitrary"), name="moe_scatter",
    )(row_start, cnt, tile_nsub, pos.reshape(steps, 1, TOP_K * SCATTER_ROWS), x)


def _combine_kernel(pos_ref, pos_next_ref, h_ref, gates_ref, y_hbm, *rest, split_steps):
    if split_steps is None:
        o_ref, ya_ref, yb_ref, sem = rest
    else:
        gfin_ref, op_ref, os_ref, ya_ref, yb_ref, sem = rest
    i = pl.program_id(0)
    slot = i % 2

    def fetch(p_ref, s):
        def body(q, carry):
            for u in range(DMA_UNROLL):
                r = q * DMA_UNROLL + u
                _row_copy(y_hbm, ya_ref.at[s], p_ref[0, 0, TOP_K * r], r, sem.at[0, s]).start()
                _row_copy(y_hbm, yb_ref.at[s], p_ref[0, 0, TOP_K * r + 1], r, sem.at[1, s]).start(priority=1)
            return carry
        lax.fori_loop(0, COMBINE_ROWS // DMA_UNROLL, body, 0)

    @pl.when(i == 0)
    def _():
        fetch(pos_ref, 0)

    @pl.when(i + 1 < pl.num_programs(0))
    def _():
        fetch(pos_next_ref, 1 - slot)

    def drain(q, carry):
        for u in range(DMA_UNROLL):
            _row_copy(y_hbm, ya_ref.at[slot], 0, 0, sem.at[0, slot]).wait()
            _row_copy(y_hbm, yb_ref.at[slot], 0, 0, sem.at[1, slot]).wait()
        return carry

    lax.fori_loop(0, COMBINE_ROWS // DMA_UNROLL, drain, 0)
    gates = gates_ref[...]
    x = h_ref[...] + (gates[:, 0:1] * ya_ref[slot] + gates[:, 1:2] * yb_ref[slot])
    if split_steps is None:
        o_ref[...] = x
    else:
        y = _rms(x, gfin_ref[...])

        @pl.when(i < split_steps)
        def _():
            op_ref[...] = y

        @pl.when(i >= split_steps)
        def _():
            os_ref[...] = y


def _combine(h, gates, pos, y, *, final_norm=None, split_rows=None):
    t, d = h.shape
    rows = COMBINE_ROWS
    steps = t // rows
    pos3 = pos.reshape(steps, 1, TOP_K * rows)
    in_specs = [pl.BlockSpec((1, 1, TOP_K * rows), lambda i: (i, 0, 0), memory_space=pltpu.SMEM),
                pl.BlockSpec((1, 1, TOP_K * rows), lambda i: (jnp.minimum(i + 1, steps - 1), 0, 0),
                             memory_space=pltpu.SMEM),
                pl.BlockSpec((rows, d), lambda i: (i, 0)),
                pl.BlockSpec((rows, LANES), lambda i: (i, 0)),
                pl.BlockSpec(memory_space=pl.ANY)]
    args = [pos3, pos3, h, gates, y]
    if final_norm is None:
        split_steps = None
        out_specs = pl.BlockSpec((rows, d), lambda i: (i, 0))
        out_shape = jax.ShapeDtypeStruct((t, d), f32)
    else:
        split_steps = split_rows // rows
        in_specs.append(pl.BlockSpec((1, d), lambda i: (0, 0)))
        args.append(final_norm.reshape(1, d))
        out_specs = [pl.BlockSpec((rows, d), lambda i: (jnp.minimum(i, split_steps - 1), 0)),
                     pl.BlockSpec((rows, d), lambda i: (jnp.maximum(i - split_steps, 0), 0))]
        out_shape = [jax.ShapeDtypeStruct((split_rows, d), f32), jax.ShapeDtypeStruct((t - split_rows, d), f32)]
    return pl.pallas_call(
        functools.partial(_combine_kernel, split_steps=split_steps),
        grid=(steps,), in_specs=in_specs, out_specs=out_specs, out_shape=out_shape,
        scratch_shapes=[pltpu.VMEM((2, rows, d), f32), pltpu.VMEM((2, rows, d), f32),
                        pltpu.SemaphoreType.DMA((2, 2))],
        compiler_params=_params("arbitrary"), name="moe_combine" if final_norm is None else "moe_combine_final",
    )(*args)


def kernel(x_prompt, x_sample, state_conv, state_pool, norm_mix, norm_ffn, w_in, conv_w, pool_w, pool_scale,
           w_out, dense_w1, dense_w3, dense_w2, router_w, moe_w1, moe_w3, moe_w2, final_norm):
    n_p, len_p, d = x_prompt.shape
    n_s, len_s, _ = x_sample.shape
    t_p, t_s = n_p * len_p, n_s * len_s
    t = t_p + t_s
    depth = w_in.shape[0]
    n_moe, n_exp = moe_w1.shape[0], moe_w1.shape[1]
    x = (x_prompt.reshape(t_p, d), x_sample.reshape(t_s, d))
    router_pad = jnp.pad(router_w, ((0, 0), (0, 0), (0, LANES - router_w.shape[2])))
    moe_w1f = moe_w1.reshape((n_moe * n_exp,) + moe_w1.shape[2:])
    moe_w3f = moe_w3.reshape((n_moe * n_exp,) + moe_w3.shape[2:])
    moe_w2f = moe_w2.reshape((n_moe * n_exp,) + moe_w2.shape[2:])
    w_in_b, w_out_b = _to_bf16(w_in), _to_bf16(w_out)

    dense_tiles = t // MLP_ROWS
    dense_blk = jnp.arange(dense_tiles, dtype=i32)
    dense_nsub = jnp.full((dense_tiles,), MLP_NSUB, i32)
    moe_tiles = (TOP_K * t) // MLP_ROWS + n_exp

    conv_p, pool_p, conv_s, pool_s = [], [], [], []
    for l in range(depth):
        i = l // 2
        is_moe = l % 2 == 1
        outs = _block(x, l, norm_mix, w_in_b, w_out_b, state_conv, state_pool, conv_w, pool_w, pool_scale, norm_ffn,
                      router_pad if is_moe else None, i,
                      n_prompt_seq=n_p, prompt_len=len_p, n_sample_seq=n_s, sample_len=len_s)
        conv_p.append(outs[-4]); pool_p.append(outs[-3]); conv_s.append(outs[-2]); pool_s.append(outs[-1])
        if is_moe:
            h, hn, gates, idx = outs[:4]
            pos, row_start, cnt, tile_e, tile_nsub, tile_blk = _route_layout(idx, moe_tiles)
            xs = _scatter_rows(hn, pos, row_start, cnt, tile_nsub)
            ys = _mlp(xs, moe_w1f, moe_w3f, moe_w2f, tile_e + i * n_exp, tile_nsub, tile_blk)
            if l == depth - 1:
                y_prompt, y_sample = _combine(h, gates, pos, ys, final_norm=final_norm, split_rows=t_p)
            else:
                x = _combine(h, gates, pos, ys)
        else:
            h, hn = outs[:2]
            x = _mlp(hn, dense_w1, dense_w3, dense_w2, jnp.full((dense_tiles,), i, i32), dense_nsub, dense_blk, res=h)
            if l == depth - 1:
                y_prompt = _rmsnorm(x, final_norm, row_offset=0, rows=t_p)
                y_sample = _rmsnorm(x, final_norm, row_offset=t_p, rows=t_s)

    return (y_prompt.reshape(n_p, len_p, d), y_sample.reshape(n_s, len_s, d),
            jnp.stack(conv_p), jnp.stack(pool_p), jnp.stack(conv_s), jnp.stack(pool_s))
```

```python
import functools

import jax
import jax.numpy as jnp
from jax import lax
from jax.experimental import pallas as pl
from jax.experimental.pallas import tpu as pltpu

f32 = jnp.float32
bf16 = jnp.bfloat16
i32 = jnp.int32

CONV_DIM = 1024
POOL_DIM = 1024
IN_DIM = 3 * CONV_DIM + POOL_DIM
CONV_W = 3
POOL_WINDOWS = (2, 4, 8, 16)
POOL_GROUP_DIM = POOL_DIM // len(POOL_WINDOWS)
POOL_BUF = max(POOL_WINDOWS) - 1
N_EXPERTS = 8
TOP_K = 2
PAST_LEN = 16384
EPS = 1e-6

LANES = 128
HIST = 16
CONV_HIST = 8
MIX_ROWS = 256
SAMPLE_SEQS = 8
CAST_ROWS = 512
MLP_ROWS = 1024
MLP_SUB = 256
MLP_NSUB = MLP_ROWS // MLP_SUB
MLP_FC = 256
SCATTER_ROWS = 256
COMBINE_ROWS = 256
DMA_UNROLL = 8
VMEM_LIMIT = 60000 * 1024


def _params(*sem):
    return pltpu.CompilerParams(dimension_semantics=sem, vmem_limit_bytes=VMEM_LIMIT)


def _rms(x, g):
    return x * lax.rsqrt(jnp.mean(x * x, axis=-1, keepdims=True) + EPS) * g


def _norm_kernel(x_ref, g_ref, o_ref):
    o_ref[...] = _rms(x_ref[...], g_ref[...])


def _rmsnorm(x, g, *, row_offset, rows, tm=512):
    d = x.shape[1]
    off = row_offset // tm
    return pl.pallas_call(
        _norm_kernel, grid=(rows // tm,),
        in_specs=[pl.BlockSpec((tm, d), lambda i: (i + off, 0)), pl.BlockSpec((1, d), lambda i: (0, 0))],
        out_specs=pl.BlockSpec((tm, d), lambda i: (i, 0)),
        out_shape=jax.ShapeDtypeStruct((rows, d), f32),
        compiler_params=_params("arbitrary"), name="rmsnorm",
    )(x, g.reshape(1, d))


def _cast_kernel(w_ref, o_ref):
    o_ref[...] = w_ref[...].astype(o_ref.dtype)


def _to_bf16(w):
    l, k, n = w.shape
    return pl.pallas_call(
        _cast_kernel, grid=(l, k // CAST_ROWS),
        in_specs=[pl.BlockSpec((None, CAST_ROWS, n), lambda a, b: (a, b, 0))],
        out_specs=pl.BlockSpec((None, CAST_ROWS, n), lambda a, b: (a, b, 0)),
        out_shape=jax.ShapeDtypeStruct(w.shape, bf16),
        compiler_params=_params("arbitrary", "arbitrary"), name="cast_bf16",
    )(w)


def _block_kernel(*refs, with_router, split_x, prompt_steps, steps_per_seq, sample_len):
    if split_x:
        xp_ref, xs_ref = refs[:2]
        refs = refs[2:]
    else:
        xp_ref = xs_ref = refs[0]
        refs = refs[1:]
    gm_ref, win_ref, wout_ref, hv_ref, hu_ref, cw_ref, pw_ref, ps_ref, gf_ref = refs[:9]
    rest = refs[9:]
    if with_router:
        rw_ref, h_ref, hn_ref, gates_ref, idx_ref = rest[:5]
        rest = rest[5:]
    else:
        h_ref, hn_ref = rest[:2]
        rest = rest[2:]
    ncp_ref, npp_ref, ncs_ref, nps_ref, proj_ref, ycat_ref, ev_ref, eu_ref, ev3_ref, eu3_ref = rest
    i = pl.program_id(0)
    rows = MIX_ROWS
    n_groups = len(POOL_WINDOWS)

    c1, c2, c3 = CONV_DIM, 2 * CONV_DIM, 3 * CONV_DIM

    def conv_taps(ext):
        conv = cw_ref[0:1, :] * pltpu.roll(ext, 2, axis=0)
        conv = conv + cw_ref[1:2, :] * pltpu.roll(ext, 1, axis=0)
        return conv + cw_ref[2:3, :] * ext

    def window_sum(ext, w):
        s, k = ext, 1
        while k < w:
            s = s + pltpu.roll(s, k, axis=0)
            k *= 2
        return s

    def full_window_mean(g, s):
        w = POOL_WINDOWS[g]
        assert w & (w - 1) == 0, "scaling by 1/w equals dividing by w only for a power of two"
        return s * (1.0 / w)

    def pool_group_out(r0, n, g, mean, u):
        lo, hi = g * POOL_GROUP_DIM, (g + 1) * POOL_GROUP_DIM
        z = jnp.dot((mean - u).astype(bf16), pw_ref[g].astype(bf16), preferred_element_type=f32)
        ycat_ref[r0:r0 + n, c1 + lo:c1 + hi] = (z * ps_ref[:, lo:hi]).astype(bf16)

    def epilogue(x, mixed):
        h = x + mixed
        h_ref[...] = h
        hn = _rms(h, gf_ref[...])
        hn_ref[...] = hn.astype(hn_ref.dtype)
        if with_router:
            rw = rw_ref[...]
            rw_hi = rw.astype(bf16)
            rw_lo = (rw - rw_hi.astype(f32)).astype(bf16)
            hn_hi = hn.astype(bf16)
            hn_lo = (hn - hn_hi.astype(f32)).astype(bf16)
            half = rows // 2
            logits = jnp.concatenate([
                jnp.dot(hn_hi[r0:r0 + half], rw_hi, preferred_element_type=f32)
                + (jnp.dot(hn_lo[r0:r0 + half], rw_hi, preferred_element_type=f32)
                   + jnp.dot(hn_hi[r0:r0 + half], rw_lo, preferred_element_type=f32))
                for r0 in (0, half)], axis=0)
            lane = lax.broadcasted_iota(i32, logits.shape, 1)
            neg = jnp.float32(-jnp.inf)
            logits = jnp.where(lane < N_EXPERTS, logits, neg)
            m1 = jnp.max(logits, axis=-1, keepdims=True)
            i1 = jnp.min(jnp.where(logits == m1, lane, LANES), axis=-1, keepdims=True)
            rest_l = jnp.where(lane == i1, neg, logits)
            m2 = jnp.max(rest_l, axis=-1, keepdims=True)
            i2 = jnp.min(jnp.where(rest_l == m2, lane, LANES), axis=-1, keepdims=True)
            e = jnp.exp(m2 - m1)
            denom = 1.0 + e
            gates_ref[...] = jnp.where(lane == 0, 1.0 / denom, jnp.where(lane == 1, e / denom, 0.0))
            idx_ref[...] = jnp.where(lane == 0, i1, jnp.where(lane == 1, i2, 0))

    @pl.when(i < prompt_steps)
    def _prompt():
        t = i % steps_per_seq

        @pl.when(t == 0)
        def _():
            ev_ref[0:HIST, :] = jnp.zeros((HIST, CONV_DIM), f32)
            eu_ref[0:HIST, :] = jnp.zeros((HIST, POOL_DIM), f32)

        @pl.when(t > 0)
        def _():
            ev_ref[0:HIST, :] = ev_ref[rows:rows + HIST, :]
            eu_ref[0:HIST, :] = eu_ref[rows:rows + HIST, :]

        x = xp_ref[...]
        xn = _rms(x, gm_ref[...]).astype(bf16)
        ch = jnp.dot(xn, win_ref[:, c1:c3], preferred_element_type=f32)
        ev_ref[HIST:, :] = ch[:, 0:CONV_DIM] * ch[:, CONV_DIM:]
        b_gate = jnp.dot(xn, win_ref[:, 0:c1], preferred_element_type=f32)
        eu_ref[HIST:, :] = jnp.dot(xn, win_ref[:, c3:], preferred_element_type=f32)
        ycat_ref[:, 0:c1] = (b_gate * conv_taps(ev_ref[...])[HIST:, :]).astype(bf16)
        mixed = jnp.dot(ycat_ref[:, 0:c1], wout_ref[0:c1, :], preferred_element_type=f32)
        head = max(POOL_WINDOWS)
        assert rows >= head
        pos1 = lax.broadcasted_iota(i32, (head, 1), 0) + 1
        for g, w in enumerate(POOL_WINDOWS):
            ext = eu_ref[:, g * POOL_GROUP_DIM:(g + 1) * POOL_GROUP_DIM]
            s = window_sum(ext, w)[HIST:, :]
            filling = s[0:head] / jnp.minimum(pos1, w).astype(f32)
            mean = jnp.concatenate([jnp.where(t == 0, filling, full_window_mean(g, s[0:head])),
                                    full_window_mean(g, s[head:])], axis=0)
            pool_group_out(0, rows, g, mean, ext[HIST:, :])
        mixed = mixed + jnp.dot(ycat_ref[:, c1:], wout_ref[c1:, :], preferred_element_type=f32)
        epilogue(x, mixed)

        @pl.when(t == steps_per_seq - 1)
        def _():
            ncp_ref[0] = ev_ref[HIST + rows - (CONV_W - 1):HIST + rows, :]
            npp_ref[0] = eu_ref[HIST + rows - POOL_BUF:HIST + rows, :]

    @pl.when(i >= prompt_steps)
    def _sample():
        s, new = SAMPLE_SEQS, sample_len
        n = s * new
        nv, nu = s * (CONV_HIST + new), s * (HIST + new)
        assert PAST_LEN + 1 >= max(POOL_WINDOWS)
        assert new >= CONV_W - 1
        x = xs_ref[...]
        xn = _rms(x, gm_ref[...]).astype(bf16)
        proj_ref[...] = jnp.dot(xn, win_ref[...], preferred_element_type=f32)
        for part in range(rows // n):
            r0, s0 = part * n, part * s
            v3 = (proj_ref[r0:r0 + n, c1:c2] * proj_ref[r0:r0 + n, c2:c3]).reshape(s, new, CONV_DIM)
            ev3_ref[:, 0:CONV_HIST - (CONV_W - 1), :] = jnp.zeros((s, CONV_HIST - (CONV_W - 1), CONV_DIM), f32)
            ev3_ref[:, CONV_HIST - (CONV_W - 1):CONV_HIST, :] = hv_ref[s0:s0 + s]
            ev3_ref[:, CONV_HIST:, :] = v3
            eu3_ref[:, 0:HIST - POOL_BUF, :] = jnp.zeros((s, HIST - POOL_BUF, POOL_DIM), f32)
            eu3_ref[:, HIST - POOL_BUF:HIST, :] = hu_ref[s0:s0 + s]
            eu3_ref[:, HIST:, :] = proj_ref[r0:r0 + n, 3 * CONV_DIM:].reshape(s, new, POOL_DIM)
            conv = conv_taps(ev3_ref[...].reshape(nv, CONV_DIM))
            conv = conv.reshape(s, CONV_HIST + new, CONV_DIM)[:, CONV_HIST:, :].reshape(n, CONV_DIM)
            ycat_ref[r0:r0 + n, 0:c1] = (proj_ref[r0:r0 + n, 0:c1] * conv).astype(bf16)
            for g, w in enumerate(POOL_WINDOWS):
                lo, hi = g * POOL_GROUP_DIM, (g + 1) * POOL_GROUP_DIM
                sg = window_sum(eu3_ref[:, :, lo:hi].reshape(nu, POOL_GROUP_DIM), w)
                sg = sg.reshape(s, HIST + new, POOL_GROUP_DIM)[:, HIST:, :].reshape(n, POOL_GROUP_DIM)
                pool_group_out(r0, n, g, full_window_mean(g, sg), proj_ref[r0:r0 + n, c3 + lo:c3 + hi])
            ncs_ref[s0:s0 + s] = v3[:, new - (CONV_W - 1):, :]
            nps_ref[s0:s0 + s] = eu3_ref[:, HIST + new - POOL_BUF:, :]
        epilogue(x, jnp.dot(ycat_ref[...], wout_ref[...], preferred_element_type=f32))


def _block(x, layer, norm_mix, w_in_b, w_out_b, hv, hu, conv_w, pool_w, pool_scale, norm_ffn, router_w, moe_layer,
           *, n_prompt_seq, prompt_len, n_sample_seq, sample_len):
    split_x = isinstance(x, tuple)
    d = (x[0] if split_x else x).shape[1]
    t_all = sum(a.shape[0] for a in x) if split_x else x.shape[0]
    rows = MIX_ROWS
    steps_per_seq = prompt_len // rows
    prompt_steps = n_prompt_seq * steps_per_seq
    seqs_per_step = rows // sample_len
    sample_steps = n_sample_seq // seqs_per_step
    last_p = n_prompt_seq - 1
    n_groups = len(POOL_WINDOWS)
    with_router = router_w is not None
    once = pl.Buffered(1)

    def samp(i):
        return jnp.maximum(i - prompt_steps, 0)

    def pseq(i):
        return jnp.minimum(i // steps_per_seq, last_p)

    if split_x:
        x_specs = [pl.BlockSpec((rows, d), lambda i: (jnp.minimum(i, prompt_steps - 1), 0)),
                   pl.BlockSpec((rows, d), lambda i: (samp(i), 0))]
        x_args = list(x)
    else:
        x_specs = [pl.BlockSpec((rows, d), lambda i: (i, 0))]
        x_args = [x]
    in_specs = x_specs + [
                pl.BlockSpec((None, 1, d), lambda i: (layer, 0, 0)),
                pl.BlockSpec((None, d, IN_DIM), lambda i: (layer, 0, 0), pipeline_mode=once),
                pl.BlockSpec((None, CONV_DIM + POOL_DIM, d), lambda i: (layer, 0, 0), pipeline_mode=once),
                pl.BlockSpec((None, seqs_per_step, CONV_W - 1, CONV_DIM), lambda i: (layer, samp(i), 0, 0),
                             pipeline_mode=once),
                pl.BlockSpec((None, seqs_per_step, POOL_BUF, POOL_DIM), lambda i: (layer, samp(i), 0, 0),
                             pipeline_mode=once),
                pl.BlockSpec((None, CONV_W, CONV_DIM), lambda i: (layer, 0, 0)),
                pl.BlockSpec((None, n_groups, POOL_GROUP_DIM, POOL_GROUP_DIM), lambda i: (layer, 0, 0, 0)),
                pl.BlockSpec((None, 1, POOL_DIM), lambda i: (layer, 0, 0)),
                pl.BlockSpec((None, 1, d), lambda i: (layer, 0, 0))]
    args = x_args + [norm_mix.reshape(-1, 1, d), w_in_b, w_out_b, hv, hu, conv_w, pool_w,
                     pool_scale.reshape(-1, 1, POOL_DIM), norm_ffn.reshape(-1, 1, d)]
    out_specs = [pl.BlockSpec((rows, d), lambda i: (i, 0)), pl.BlockSpec((rows, d), lambda i: (i, 0))]
    out_shape = [jax.ShapeDtypeStruct((t_all, d), f32),
                 jax.ShapeDtypeStruct((t_all, d), f32 if with_router else bf16)]
    if with_router:
        in_specs.append(pl.BlockSpec((None, d, LANES), lambda i: (moe_layer, 0, 0)))
        args.append(router_w)
        out_specs += [pl.BlockSpec((rows, LANES), lambda i: (i, 0)), pl.BlockSpec((rows, LANES), lambda i: (i, 0))]
        out_shape += [jax.ShapeDtypeStruct((t_all, LANES), f32), jax.ShapeDtypeStruct((t_all, LANES), i32)]
    out_specs += [pl.BlockSpec((1, CONV_W - 1, CONV_DIM), lambda i: (pseq(i), 0, 0)),
                  pl.BlockSpec((1, POOL_BUF, POOL_DIM), lambda i: (pseq(i), 0, 0)),
                  pl.BlockSpec((seqs_per_step, CONV_W - 1, CONV_DIM), lambda i: (samp(i), 0, 0), pipeline_mode=once),
                  pl.BlockSpec((seqs_per_step, POOL_BUF, POOL_DIM), lambda i: (samp(i), 0, 0), pipeline_mode=once)]
    out_shape += [jax.ShapeDtypeStruct((n_prompt_seq, CONV_W - 1, CONV_DIM), f32),
                  jax.ShapeDtypeStruct((n_prompt_seq, POOL_BUF, POOL_DIM), f32),
                  jax.ShapeDtypeStruct((n_sample_seq, CONV_W - 1, CONV_DIM), f32),
                  jax.ShapeDtypeStruct((n_sample_seq, POOL_BUF, POOL_DIM), f32)]
    kern = functools.partial(_block_kernel, with_router=with_router, split_x=split_x, prompt_steps=prompt_steps,
                             steps_per_seq=steps_per_seq, sample_len=sample_len)
    return pl.pallas_call(
        kern, grid=(prompt_steps + sample_steps,), in_specs=in_specs, out_specs=out_specs, out_shape=out_shape,
        scratch_shapes=[pltpu.VMEM((rows, IN_DIM), f32), pltpu.VMEM((rows, CONV_DIM + POOL_DIM), bf16),
                        pltpu.VMEM((HIST + rows, CONV_DIM), f32), pltpu.VMEM((HIST + rows, POOL_DIM), f32),
                        pltpu.VMEM((SAMPLE_SEQS, CONV_HIST + sample_len, CONV_DIM), f32),
                        pltpu.VMEM((SAMPLE_SEQS, HIST + sample_len, POOL_DIM), f32)],
        compiler_params=_params("arbitrary"), name="block_router" if with_router else "block",
    )(*args)


def _mlp_kernel(te_ref, ns_ref, tb_ref, x_ref, w1_ref, w3_ref, w2_ref, *rest, grouped):
    del te_ref, tb_ref
    if grouped:
        o_ref, xb_ref, w1b_ref, w3b_ref, w2b_ref = rest
    else:
        res_ref, o_ref, w1b_ref, w3b_ref, w2b_ref = rest
    g, c = pl.program_id(0), pl.program_id(1)
    nsub = ns_ref[g]
    xsrc = xb_ref if grouped else x_ref

    def cast_weights():
        w1b_ref[...] = w1_ref[...].astype(bf16)
        w3b_ref[...] = w3_ref[...].astype(bf16)
        w2b_ref[...] = w2_ref[...].astype(bf16)

    def chunk(x):
        h1 = jnp.dot(x, w1b_ref[...], preferred_element_type=f32)
        h3 = jnp.dot(x, w3b_ref[...], preferred_element_type=f32)
        hid = (h1 * jax.nn.sigmoid(h1) * h3).astype(bf16)
        return jnp.dot(hid, w2b_ref[...], preferred_element_type=f32)

    @pl.when(c == 0)
    def _init():
        if grouped:
            o_ref[...] = jnp.zeros_like(o_ref)

            @pl.when(nsub > 0)
            def _():
                xb_ref[...] = x_ref[...].astype(bf16)
        else:
            o_ref[...] = res_ref[...]

    for k in range(1 if grouped else MLP_NSUB, MLP_NSUB + 1):
        @pl.when(nsub == k)
        def _():
            m = k * MLP_SUB
            cast_weights()
            o_ref[0:m, :] += chunk(xsrc[0:m, :])


def _mlp(x, w1, w3, w2, tile_expert, tile_nsub, tile_blk, *, res=None):
    grouped = res is None
    k = x.shape[1]
    f, n = w2.shape[1], w2.shape[2]
    n_tiles = tile_expert.shape[0]
    fc = MLP_FC
    n_chunks = f // fc

    def wcol(g, c, te, ns, tb):
        return jnp.where(ns[g] > 0, c, n_chunks - 1)

    in_specs = [
        pl.BlockSpec((MLP_ROWS, k), lambda g, c, te, ns, tb: (tb[g], 0)),
        pl.BlockSpec((None, k, fc), lambda g, c, te, ns, tb: (te[g], 0, wcol(g, c, te, ns, tb))),
        pl.BlockSpec((None, k, fc), lambda g, c, te, ns, tb: (te[g], 0, wcol(g, c, te, ns, tb))),
        pl.BlockSpec((None, fc, n), lambda g, c, te, ns, tb: (te[g], wcol(g, c, te, ns, tb), 0)),
    ]
    args = [x, w1, w3, w2]
    scratch = [pltpu.VMEM((k, fc), bf16), pltpu.VMEM((k, fc), bf16), pltpu.VMEM((fc, n), bf16)]
    if grouped:
        scratch = [pltpu.VMEM((MLP_ROWS, k), bf16)] + scratch
    else:
        in_specs.append(pl.BlockSpec((MLP_ROWS, n), lambda g, c, te, ns, tb: (tb[g], 0)))
        args.append(res)
    return pl.pallas_call(
        functools.partial(_mlp_kernel, grouped=grouped),
        grid_spec=pltpu.PrefetchScalarGridSpec(
            num_scalar_prefetch=3, grid=(n_tiles, n_chunks), in_specs=in_specs,
            out_specs=pl.BlockSpec((MLP_ROWS, n), lambda g, c, te, ns, tb: (g, 0)),
            scratch_shapes=scratch),
        out_shape=jax.ShapeDtypeStruct((x.shape[0], n), f32),
        compiler_params=_params("arbitrary", "arbitrary"), name="mlp_grouped" if grouped else "mlp_dense",
    )(tile_expert, tile_nsub, tile_blk, *args)


def _route_layout(idx, n_tiles):
    t = idx.shape[0]
    flat_e = idx[:, :TOP_K].reshape(-1)
    onehot = (flat_e[:, None] == jnp.arange(N_EXPERTS, dtype=i32)[None, :]).astype(i32)
    csum = jnp.cumsum(onehot, axis=0)
    rank = jnp.sum(onehot * csum, axis=1) - 1
    cnt = csum[-1]
    tiles_e = (cnt + MLP_ROWS - 1) // MLP_ROWS
    tile_end = jnp.cumsum(tiles_e)
    tile_start = tile_end - tiles_e
    pos = jnp.sum(onehot * tile_start[None, :], axis=1) * MLP_ROWS + rank
    n_used = tile_end[-1]
    g = jnp.arange(n_tiles, dtype=i32)
    g_eff = jnp.minimum(g, n_used - 1)
    tile_e = jnp.minimum(jnp.sum((tile_end[None, :] <= g_eff[:, None]).astype(i32), axis=1), N_EXPERTS - 1)
    tile_oh = (tile_e[:, None] == jnp.arange(N_EXPERTS, dtype=i32)[None, :]).astype(i32)
    rows_in = jnp.clip(jnp.sum(tile_oh * cnt[None, :], axis=1)
                       - (g_eff - jnp.sum(tile_oh * tile_start[None, :], axis=1)) * MLP_ROWS, 0, MLP_ROWS)
    nsub = jnp.where(g < n_used, (rows_in + MLP_SUB - 1) // MLP_SUB, 0).astype(i32)
    return pos.reshape(t, TOP_K), tile_start * MLP_ROWS, cnt, tile_e, nsub, g_eff.astype(i32)


def _row_copy(src_hbm, dst_ref, src_row, dst_row, sem):
    return pltpu.make_async_copy(src_hbm.at[pl.ds(src_row, 1)], dst_ref.at[pl.ds(dst_row, 1)], sem)


def _scatter_kernel(start_ref, cnt_ref, ns_ref, pos_ref, x_ref, o_hbm, zeros_ref, sem, zsem):
    i = pl.program_id(0)
    last = pl.num_programs(0) - 1
    n_tiles = o_hbm.shape[0] // MLP_ROWS

    def zero_fill(act):
        for g in range(n_tiles):
            for k in range(MLP_NSUB):
                @pl.when(ns_ref[g] <= k)
                def _():
                    r0 = g * MLP_ROWS + k * MLP_SUB
                    act(pltpu.make_async_copy(zeros_ref, o_hbm.at[pl.ds(r0, MLP_SUB)], zsem))
        for e in range(N_EXPERTS):
            tail = (-cnt_ref[e]) & (MLP_SUB - 1)
            first = start_ref[e] + cnt_ref[e]

            def one_row(j, carry):
                act(_row_copy(zeros_ref, o_hbm, 0, first + j, zsem))
                return carry

            lax.fori_loop(0, tail, one_row, 0)

    @pl.when(i == 0)
    def _():
        zeros_ref[...] = jnp.zeros_like(zeros_ref)
        zero_fill(lambda cp: cp.start())

    def issue(q, carry):
        r0 = pl.multiple_of(q * DMA_UNROLL, DMA_UNROLL)
        for u in range(DMA_UNROLL):
            for k in range(TOP_K):
                _row_copy(x_ref, o_hbm, r0 + u, pos_ref[0, 0, TOP_K * (r0 + u) + k], sem.at[k]).start(priority=k)
        return carry

    def drain(q, carry):
        for u in range(DMA_UNROLL):
            for k in range(TOP_K):
                _row_copy(x_ref, o_hbm, 0, 0, sem.at[k]).wait()
        return carry

    lax.fori_loop(0, SCATTER_ROWS // DMA_UNROLL, issue, 0)
    lax.fori_loop(0, SCATTER_ROWS // DMA_UNROLL, drain, 0)

    @pl.when(i == last)
    def _():
        zero_fill(lambda cp: cp.wait())


def _scatter_rows(x, pos, row_start, cnt, tile_nsub):
    t, d = x.shape
    n_tiles = tile_nsub.shape[0]
    steps = t // SCATTER_ROWS
    return pl.pallas_call(
        _scatter_kernel,
        grid_spec=pltpu.PrefetchScalarGridSpec(
            num_scalar_prefetch=3, grid=(steps,),
            in_specs=[pl.BlockSpec((1, 1, TOP_K * SCATTER_ROWS), lambda i, *_: (i, 0, 0), memory_space=pltpu.SMEM),
                      pl.BlockSpec((SCATTER_ROWS, d), lambda i, *_: (i, 0))],
            out_specs=pl.BlockSpec(memory_space=pl.ANY),
            scratch_shapes=[pltpu.VMEM((MLP_SUB, d), x.dtype), pltpu.SemaphoreType.DMA((TOP_K,)),
                            pltpu.SemaphoreType.DMA(())]),
        out_shape=jax.ShapeDtypeStruct((n_tiles * MLP_ROWS, d), x.dtype),
        compiler_params=_params("arbitrary"), name="moe_scatter",
    )(row_start, cnt, tile_nsub, pos.reshape(steps, 1, TOP_K * SCATTER_ROWS), x)


def _combine_kernel(pos_ref, pos_next_ref, h_ref, gates_ref, y_hbm, *rest, split_steps):
    if split_steps is None:
        o_ref, ya_ref, yb_ref, sem = rest
    else:
        gfin_ref, op_ref, os_ref, ya_ref, yb_ref, sem = rest
    i = pl.program_id(0)
    slot = i % 2

    def fetch(p_ref, s):
        def body(q, carry):
            for u in range(DMA_UNROLL):
                r = q * DMA_UNROLL + u
                _row_copy(y_hbm, ya_ref.at[s], p_ref[0, 0, TOP_K * r], r, sem.at[0, s]).start()
                _row_copy(y_hbm, yb_ref.at[s], p_ref[0, 0, TOP_K * r + 1], r, sem.at[1, s]).start(priority=1)
            return carry
        lax.fori_loop(0, COMBINE_ROWS // DMA_UNROLL, body, 0)

    @pl.when(i == 0)
    def _():
        fetch(pos_ref, 0)

    @pl.when(i + 1 < pl.num_programs(0))
    def _():
        fetch(pos_next_ref, 1 - slot)

    def drain(q, carry):
        for u in range(DMA_UNROLL):
            _row_copy(y_hbm, ya_ref.at[slot], 0, 0, sem.at[0, slot]).wait()
            _row_copy(y_hbm, yb_ref.at[slot], 0, 0, sem.at[1, slot]).wait()
        return carry

    lax.fori_loop(0, COMBINE_ROWS // DMA_UNROLL, drain, 0)
    gates = gates_ref[...]
    x = h_ref[...] + (gates[:, 0:1] * ya_ref[slot] + gates[:, 1:2] * yb_ref[slot])
    if split_steps is None:
        o_ref[...] = x
    else:
        y = _rms(x, gfin_ref[...])

        @pl.when(i < split_steps)
        def _():
            op_ref[...] = y

        @pl.when(i >= split_steps)
        def _():
            os_ref[...] = y


def _combine(h, gates, pos, y, *, final_norm=None, split_rows=None):
    t, d = h.shape
    rows = COMBINE_ROWS
    steps = t // rows
    pos3 = pos.reshape(steps, 1, TOP_K * rows)
    in_specs = [pl.BlockSpec((1, 1, TOP_K * rows), lambda i: (i, 0, 0), memory_space=pltpu.SMEM),
                pl.BlockSpec((1, 1, TOP_K * rows), lambda i: (jnp.minimum(i + 1, steps - 1), 0, 0),
                             memory_space=pltpu.SMEM),
                pl.BlockSpec((rows, d), lambda i: (i, 0)),
                pl.BlockSpec((rows, LANES), lambda i: (i, 0)),
                pl.BlockSpec(memory_space=pl.ANY)]
    args = [pos3, pos3, h, gates, y]
    if final_norm is None:
        split_steps = None
        out_specs = pl.BlockSpec((rows, d), lambda i: (i, 0))
        out_shape = jax.ShapeDtypeStruct((t, d), f32)
    else:
        split_steps = split_rows // rows
        in_specs.append(pl.BlockSpec((1, d), lambda i: (0, 0)))
        args.append(final_norm.reshape(1, d))
        out_specs = [pl.BlockSpec((rows, d), lambda i: (jnp.minimum(i, split_steps - 1), 0)),
                     pl.BlockSpec((rows, d), lambda i: (jnp.maximum(i - split_steps, 0), 0))]
        out_shape = [jax.ShapeDtypeStruct((split_rows, d), f32), jax.ShapeDtypeStruct((t - split_rows, d), f32)]
    return pl.pallas_call(
        functools.partial(_combine_kernel, split_steps=split_steps),
        grid=(steps,), in_specs=in_specs, out_specs=out_specs, out_shape=out_shape,
        scratch_shapes=[pltpu.VMEM((2, rows, d), f32), pltpu.VMEM((2, rows, d), f32),
                        pltpu.SemaphoreType.DMA((2, 2))],
        compiler_params=_params("arbitrary"), name="moe_combine" if final_norm is None else "moe_combine_final",
    )(*args)


def kernel(x_prompt, x_sample, state_conv, state_pool, norm_mix, norm_ffn, w_in, conv_w, pool_w, pool_scale,
           w_out, dense_w1, dense_w3, dense_w2, router_w, moe_w1, moe_w3, moe_w2, final_norm):
    n_p, len_p, d = x_prompt.shape
    n_s, len_s, _ = x_sample.shape
    t_p, t_s = n_p * len_p, n_s * len_s
    t = t_p + t_s
    depth = w_in.shape[0]
    n_moe, n_exp = moe_w1.shape[0], moe_w1.shape[1]
    x = (x_prompt.reshape(t_p, d), x_sample.reshape(t_s, d))
    router_pad = jnp.pad(router_w, ((0, 0), (0, 0), (0, LANES - router_w.shape[2])))
    moe_w1f = moe_w1.reshape((n_moe * n_exp,) + moe_w1.shape[2:])
    moe_w3f = moe_w3.reshape((n_moe * n_exp,) + moe_w3.shape[2:])
    moe_w2f = moe_w2.reshape((n_moe * n_exp,) + moe_w2.shape[2:])
    w_in_b, w_out_b = _to_bf16(w_in), _to_bf16(w_out)

    dense_tiles = t // MLP_ROWS
    dense_blk = jnp.arange(dense_tiles, dtype=i32)
    dense_nsub = jnp.full((dense_tiles,), MLP_NSUB, i32)
    moe_tiles = (TOP_K * t) // MLP_ROWS + n_exp

    conv_p, pool_p, conv_s, pool_s = [], [], [], []
    for l in range(depth):
        i = l // 2
        is_moe = l % 2 == 1
        outs = _block(x, l, norm_mix, w_in_b, w_out_b, state_conv, state_pool, conv_w, pool_w, pool_scale, norm_ffn,
                      router_pad if is_moe else None, i,
                      n_prompt_seq=n_p, prompt_len=len_p, n_sample_seq=n_s, sample_len=len_s)
        conv_p.append(outs[-4]); pool_p.append(outs[-3]); conv_s.append(outs[-2]); pool_s.append(outs[-1])
        if is_moe:
            h, hn, gates, idx = outs[:4]
            pos, row_start, cnt, tile_e, tile_nsub, tile_blk = _route_layout(idx, moe_tiles)
            xs = _scatter_rows(hn, pos, row_start, cnt, tile_nsub)
            ys = _mlp(xs, moe_w1f, moe_w3f, moe_w2f, tile_e + i * n_exp, tile_nsub, tile_blk)
            if l == depth - 1:
                y_prompt, y_sample = _combine(h, gates, pos, ys, final_norm=final_norm, split_rows=t_p)
            else:
                x = _combine(h, gates, pos, ys)
        else:
            h, hn = outs[:2]
            x = _mlp(hn, dense_w1, dense_w3, dense_w2, jnp.full((dense_tiles,), i, i32), dense_nsub, dense_blk, res=h)
            if l == depth - 1:
                y_prompt = _rmsnorm(x, final_norm, row_offset=0, rows=t_p)
                y_sample = _rmsnorm(x, final_norm, row_offset=t_p, rows=t_s)

    return (y_prompt.reshape(n_p, len_p, d), y_sample.reshape(n_s, len_s, d),
            jnp.stack(conv_p), jnp.stack(pool_p), jnp.stack(conv_s), jnp.stack(pool_s))
```

```python
import functools

import jax
import jax.numpy as jnp
from jax import lax
from jax.experimental import pallas as pl
from jax.experimental.pallas import tpu as pltpu

f32 = jnp.float32
bf16 = jnp.bfloat16
i32 = jnp.int32

CONV_DIM = 1024
POOL_DIM = 1024
IN_DIM = 3 * CONV_DIM + POOL_DIM
CONV_W = 3
POOL_WINDOWS = (2, 4, 8, 16)
POOL_GROUP_DIM = POOL_DIM // len(POOL_WINDOWS)
POOL_BUF = max(POOL_WINDOWS) - 1
N_EXPERTS = 8
TOP_K = 2
PAST_LEN = 16384
EPS = 1e-6

LANES = 128
HIST = 16
CONV_HIST = 8
MIX_ROWS = 256
SAMPLE_SEQS = 8
CAST_ROWS = 512
MLP_ROWS = 1024
MLP_SUB = 256
MLP_NSUB = MLP_ROWS // MLP_SUB
MLP_FC = 256
MLP_FC_BF16 = 256
SCATTER_ROWS = 512
COMBINE_ROWS = 512
DMA_UNROLL = 8
VMEM_LIMIT = 60000 * 1024


def _params(*sem):
    return pltpu.CompilerParams(dimension_semantics=sem, vmem_limit_bytes=VMEM_LIMIT)


def _rms(x, g):
    return x * lax.rsqrt(jnp.mean(x * x, axis=-1, keepdims=True) + EPS) * g


def _norm_kernel(x_ref, g_ref, o_ref):
    o_ref[...] = _rms(x_ref[...], g_ref[...])


def _rmsnorm(x, g, *, row_offset, rows, tm=512):
    d = x.shape[1]
    off = row_offset // tm
    return pl.pallas_call(
        _norm_kernel, grid=(rows // tm,),
        in_specs=[pl.BlockSpec((tm, d), lambda i: (i + off, 0)), pl.BlockSpec((1, d), lambda i: (0, 0))],
        out_specs=pl.BlockSpec((tm, d), lambda i: (i, 0)),
        out_shape=jax.ShapeDtypeStruct((rows, d), f32),
        compiler_params=_params("arbitrary"), name="rmsnorm",
    )(x, g.reshape(1, d))


def _cast_kernel(w_ref, o_ref):
    o_ref[...] = w_ref[...].astype(o_ref.dtype)


def _to_bf16(w):
    l, k, n = w.shape
    return pl.pallas_call(
        _cast_kernel, grid=(l, k // CAST_ROWS),
        in_specs=[pl.BlockSpec((None, CAST_ROWS, n), lambda a, b: (a, b, 0))],
        out_specs=pl.BlockSpec((None, CAST_ROWS, n), lambda a, b: (a, b, 0)),
        out_shape=jax.ShapeDtypeStruct(w.shape, bf16),
        compiler_params=_params("arbitrary", "arbitrary"), name="cast_bf16",
    )(w)


def _block_kernel(*refs, with_router, split_x, prompt_steps, steps_per_seq, sample_len):
    if split_x:
        xp_ref, xs_ref = refs[:2]
        refs = refs[2:]
    else:
        xp_ref = xs_ref = refs[0]
        refs = refs[1:]
    gm_ref, win_ref, wout_ref, hv_ref, hu_ref, cw_ref, pw_ref, ps_ref, gf_ref = refs[:9]
    rest = refs[9:]
    if with_router:
        rw_ref, h_ref, hn_ref, gates_ref, idx_ref = rest[:5]
        rest = rest[5:]
    else:
        h_ref, hn_ref = rest[:2]
        rest = rest[2:]
    ncp_ref, npp_ref, ncs_ref, nps_ref, proj_ref, ycat_ref, ev_ref, eu_ref, ev3_ref, eu3_ref = rest
    i = pl.program_id(0)
    rows = MIX_ROWS
    n_groups = len(POOL_WINDOWS)

    c1, c2, c3 = CONV_DIM, 2 * CONV_DIM, 3 * CONV_DIM

    def conv_taps(ext):
        conv = cw_ref[0:1, :] * pltpu.roll(ext, 2, axis=0)
        conv = conv + cw_ref[1:2, :] * pltpu.roll(ext, 1, axis=0)
        return conv + cw_ref[2:3, :] * ext

    def window_sum(ext, w):
        s, k = ext, 1
        while k < w:
            s = s + pltpu.roll(s, k, axis=0)
            k *= 2
        return s

    def full_window_mean(g, s):
        w = POOL_WINDOWS[g]
        assert w & (w - 1) == 0, "scaling by 1/w equals dividing by w only for a power of two"
        return s * (1.0 / w)

    def pool_group_out(r0, n, g, mean, u):
        lo, hi = g * POOL_GROUP_DIM, (g + 1) * POOL_GROUP_DIM
        z = jnp.dot((mean - u).astype(bf16), pw_ref[g].astype(bf16), preferred_element_type=f32)
        ycat_ref[r0:r0 + n, c1 + lo:c1 + hi] = (z * ps_ref[:, lo:hi]).astype(bf16)

    def epilogue(x, mixed):
        h = x + mixed
        h_ref[...] = h
        hn = _rms(h, gf_ref[...])
        hn_ref[...] = hn.astype(hn_ref.dtype)
        if with_router:
            rw = rw_ref[...]
            rw_hi = rw.astype(bf16)
            rw_lo = (rw - rw_hi.astype(f32)).astype(bf16)
            hn_hi = hn.astype(bf16)
            hn_lo = (hn - hn_hi.astype(f32)).astype(bf16)
            half = rows // 2
            logits = jnp.concatenate([
                jnp.dot(hn_hi[r0:r0 + half], rw_hi, preferred_element_type=f32)
                + (jnp.dot(hn_lo[r0:r0 + half], rw_hi, preferred_element_type=f32)
                   + jnp.dot(hn_hi[r0:r0 + half], rw_lo, preferred_element_type=f32))
                for r0 in (0, half)], axis=0)
            lane = lax.broadcasted_iota(i32, logits.shape, 1)
            neg = jnp.float32(-jnp.inf)
            logits = jnp.where(lane < N_EXPERTS, logits, neg)
            m1 = jnp.max(logits, axis=-1, keepdims=True)
            i1 = jnp.min(jnp.where(logits == m1, lane, LANES), axis=-1, keepdims=True)
            rest_l = jnp.where(lane == i1, neg, logits)
            m2 = jnp.max(rest_l, axis=-1, keepdims=True)
            i2 = jnp.min(jnp.where(rest_l == m2, lane, LANES), axis=-1, keepdims=True)
            e = jnp.exp(m2 - m1)
            denom = 1.0 + e
            gates_ref[...] = jnp.where(lane == 0, 1.0 / denom, jnp.where(lane == 1, e / denom, 0.0))
            idx_ref[...] = jnp.where(lane == 0, i1, jnp.where(lane == 1, i2, 0))

    @pl.when(i < prompt_steps)
    def _prompt():
        t = i % steps_per_seq

        @pl.when(t == 0)
        def _():
            ev_ref[0:HIST, :] = jnp.zeros((HIST, CONV_DIM), f32)
            eu_ref[0:HIST, :] = jnp.zeros((HIST, POOL_DIM), f32)

        @pl.when(t > 0)
        def _():
            ev_ref[0:HIST, :] = ev_ref[rows:rows + HIST, :]
            eu_ref[0:HIST, :] = eu_ref[rows:rows + HIST, :]

        x = xp_ref[...]
        xn = _rms(x, gm_ref[...]).astype(bf16)
        ch = jnp.dot(xn, win_ref[:, c1:c3], preferred_element_type=f32)
        ev_ref[HIST:, :] = ch[:, 0:CONV_DIM] * ch[:, CONV_DIM:]
        b_gate = jnp.dot(xn, win_ref[:, 0:c1], preferred_element_type=f32)
        eu_ref[HIST:, :] = jnp.dot(xn, win_ref[:, c3:], preferred_element_type=f32)
        ycat_ref[:, 0:c1] = (b_gate * conv_taps(ev_ref[...])[HIST:, :]).astype(bf16)
        mixed = jnp.dot(ycat_ref[:, 0:c1], wout_ref[0:c1, :], preferred_element_type=f32)
        head = max(POOL_WINDOWS)
        assert rows >= head
        pos1 = lax.broadcasted_iota(i32, (head, 1), 0) + 1
        for g, w in enumerate(POOL_WINDOWS):
            ext = eu_ref[:, g * POOL_GROUP_DIM:(g + 1) * POOL_GROUP_DIM]
            s = window_sum(ext, w)[HIST:, :]
            filling = s[0:head] / jnp.minimum(pos1, w).astype(f32)
            mean = jnp.concatenate([jnp.where(t == 0, filling, full_window_mean(g, s[0:head])),
                                    full_window_mean(g, s[head:])], axis=0)
            pool_group_out(0, rows, g, mean, ext[HIST:, :])
        mixed = mixed + jnp.dot(ycat_ref[:, c1:], wout_ref[c1:, :], preferred_element_type=f32)
        epilogue(x, mixed)

        @pl.when(t == steps_per_seq - 1)
        def _():
            ncp_ref[0] = ev_ref[HIST + rows - (CONV_W - 1):HIST + rows, :]
            npp_ref[0] = eu_ref[HIST + rows - POOL_BUF:HIST + rows, :]

    @pl.when(i >= prompt_steps)
    def _sample():
        s, new = SAMPLE_SEQS, sample_len
        n = s * new
        nv, nu = s * (CONV_HIST + new), s * (HIST + new)
        assert PAST_LEN + 1 >= max(POOL_WINDOWS)
        assert new >= CONV_W - 1
        x = xs_ref[...]
        xn = _rms(x, gm_ref[...]).astype(bf16)
        proj_ref[...] = jnp.dot(xn, win_ref[...], preferred_element_type=f32)
        for part in range(rows // n):
            r0, s0 = part * n, part * s
            v3 = (proj_ref[r0:r0 + n, c1:c2] * proj_ref[r0:r0 + n, c2:c3]).reshape(s, new, CONV_DIM)
            ev3_ref[:, 0:CONV_HIST - (CONV_W - 1), :] = jnp.zeros((s, CONV_HIST - (CONV_W - 1), CONV_DIM), f32)
            ev3_ref[:, CONV_HIST - (CONV_W - 1):CONV_HIST, :] = hv_ref[s0:s0 + s]
            ev3_ref[:, CONV_HIST:, :] = v3
            eu3_ref[:, 0:HIST - POOL_BUF, :] = jnp.zeros((s, HIST - POOL_BUF, POOL_DIM), f32)
            eu3_ref[:, HIST - POOL_BUF:HIST, :] = hu_ref[s0:s0 + s]
            eu3_ref[:, HIST:, :] = proj_ref[r0:r0 + n, 3 * CONV_DIM:].reshape(s, new, POOL_DIM)
            conv = conv_taps(ev3_ref[...].reshape(nv, CONV_DIM))
            conv = conv.reshape(s, CONV_HIST + new, CONV_DIM)[:, CONV_HIST:, :].reshape(n, CONV_DIM)
            ycat_ref[r0:r0 + n, 0:c1] = (proj_ref[r0:r0 + n, 0:c1] * conv).astype(bf16)
            for g, w in enumerate(POOL_WINDOWS):
                lo, hi = g * POOL_GROUP_DIM, (g + 1) * POOL_GROUP_DIM
                sg = window_sum(eu3_ref[:, :, lo:hi].reshape(nu, POOL_GROUP_DIM), w)
                sg = sg.reshape(s, HIST + new, POOL_GROUP_DIM)[:, HIST:, :].reshape(n, POOL_GROUP_DIM)
                pool_group_out(r0, n, g, full_window_mean(g, sg), proj_ref[r0:r0 + n, c3 + lo:c3 + hi])
            ncs_ref[s0:s0 + s] = v3[:, new - (CONV_W - 1):, :]
            nps_ref[s0:s0 + s] = eu3_ref[:, HIST + new - POOL_BUF:, :]
        epilogue(x, jnp.dot(ycat_ref[...], wout_ref[...], preferred_element_type=f32))


def _block(x, layer, norm_mix, w_in_b, w_out_b, hv, hu, conv_w, pool_w, pool_scale, norm_ffn, router_w, moe_layer,
           *, n_prompt_seq, prompt_len, n_sample_seq, sample_len):
    split_x = isinstance(x, tuple)
    d = (x[0] if split_x else x).shape[1]
    t_all = sum(a.shape[0] for a in x) if split_x else x.shape[0]
    rows = MIX_ROWS
    steps_per_seq = prompt_len // rows
    prompt_steps = n_prompt_seq * steps_per_seq
    seqs_per_step = rows // sample_len
    sample_steps = n_sample_seq // seqs_per_step
    last_p = n_prompt_seq - 1
    n_groups = len(POOL_WINDOWS)
    with_router = router_w is not None
    once = pl.Buffered(1)

    def samp(i):
        return jnp.maximum(i - prompt_steps, 0)

    def pseq(i):
        return jnp.minimum(i // steps_per_seq, last_p)

    if split_x:
        x_specs = [pl.BlockSpec((rows, d), lambda i: (jnp.minimum(i, prompt_steps - 1), 0)),
                   pl.BlockSpec((rows, d), lambda i: (samp(i), 0))]
        x_args = list(x)
    else:
        x_specs = [pl.BlockSpec((rows, d), lambda i: (i, 0))]
        x_args = [x]
    in_specs = x_specs + [
                pl.BlockSpec((None, 1, d), lambda i: (layer, 0, 0)),
                pl.BlockSpec((None, d, IN_DIM), lambda i: (layer, 0, 0), pipeline_mode=once),
                pl.BlockSpec((None, CONV_DIM + POOL_DIM, d), lambda i: (layer, 0, 0), pipeline_mode=once),
                pl.BlockSpec((None, seqs_per_step, CONV_W - 1, CONV_DIM), lambda i: (layer, samp(i), 0, 0),
                             pipeline_mode=once),
                pl.BlockSpec((None, seqs_per_step, POOL_BUF, POOL_DIM), lambda i: (layer, samp(i), 0, 0),
                             pipeline_mode=once),
                pl.BlockSpec((None, CONV_W, CONV_DIM), lambda i: (layer, 0, 0)),
                pl.BlockSpec((None, n_groups, POOL_GROUP_DIM, POOL_GROUP_DIM), lambda i: (layer, 0, 0, 0)),
                pl.BlockSpec((None, 1, POOL_DIM), lambda i: (layer, 0, 0)),
                pl.BlockSpec((None, 1, d), lambda i: (layer, 0, 0))]
    args = x_args + [norm_mix.reshape(-1, 1, d), w_in_b, w_out_b, hv, hu, conv_w, pool_w,
                     pool_scale.reshape(-1, 1, POOL_DIM), norm_ffn.reshape(-1, 1, d)]
    out_specs = [pl.BlockSpec((rows, d), lambda i: (i, 0)), pl.BlockSpec((rows, d), lambda i: (i, 0))]
    out_shape = [jax.ShapeDtypeStruct((t_all, d), f32),
                 jax.ShapeDtypeStruct((t_all, d), f32 if with_router else bf16)]
    if with_router:
        in_specs.append(pl.BlockSpec((None, d, LANES), lambda i: (moe_layer, 0, 0)))
        args.append(router_w)
        out_specs += [pl.BlockSpec((rows, LANES), lambda i: (i, 0)), pl.BlockSpec((rows, LANES), lambda i: (i, 0))]
        out_shape += [jax.ShapeDtypeStruct((t_all, LANES), f32), jax.ShapeDtypeStruct((t_all, LANES), i32)]
    out_specs += [pl.BlockSpec((1, CONV_W - 1, CONV_DIM), lambda i: (pseq(i), 0, 0)),
                  pl.BlockSpec((1, POOL_BUF, POOL_DIM), lambda i: (pseq(i), 0, 0)),
                  pl.BlockSpec((seqs_per_step, CONV_W - 1, CONV_DIM), lambda i: (samp(i), 0, 0), pipeline_mode=once),
                  pl.BlockSpec((seqs_per_step, POOL_BUF, POOL_DIM), lambda i: (samp(i), 0, 0), pipeline_mode=once)]
    out_shape += [jax.ShapeDtypeStruct((n_prompt_seq, CONV_W - 1, CONV_DIM), f32),
                  jax.ShapeDtypeStruct((n_prompt_seq, POOL_BUF, POOL_DIM), f32),
                  jax.ShapeDtypeStruct((n_sample_seq, CONV_W - 1, CONV_DIM), f32),
                  jax.ShapeDtypeStruct((n_sample_seq, POOL_BUF, POOL_DIM), f32)]
    kern = functools.partial(_block_kernel, with_router=with_router, split_x=split_x, prompt_steps=prompt_steps,
                             steps_per_seq=steps_per_seq, sample_len=sample_len)
    return pl.pallas_call(
        kern, grid=(prompt_steps + sample_steps,), in_specs=in_specs, out_specs=out_specs, out_shape=out_shape,
        scratch_shapes=[pltpu.VMEM((rows, IN_DIM), f32), pltpu.VMEM((rows, CONV_DIM + POOL_DIM), bf16),
                        pltpu.VMEM((HIST + rows, CONV_DIM), f32), pltpu.VMEM((HIST + rows, POOL_DIM), f32),
                        pltpu.VMEM((SAMPLE_SEQS, CONV_HIST + sample_len, CONV_DIM), f32),
                        pltpu.VMEM((SAMPLE_SEQS, HIST + sample_len, POOL_DIM), f32)],
        compiler_params=_params("arbitrary"), name="block_router" if with_router else "block",
    )(*args)


def _mlp_kernel(te_ref, ns_ref, tb_ref, x_ref, w1_ref, w3_ref, w2_ref, *rest, grouped):
    del te_ref, tb_ref
    precast = w1_ref.dtype == bf16
    if grouped:
        o_ref, xb_ref = rest[:2]
        rest = rest[2:]
    else:
        res_ref, o_ref = rest[:2]
        rest = rest[2:]
    w1b_ref, w3b_ref, w2b_ref = (w1_ref, w3_ref, w2_ref) if precast else rest
    g, c = pl.program_id(0), pl.program_id(1)
    nsub = ns_ref[g]
    xsrc = xb_ref if grouped else x_ref

    def cast_weights():
        if not precast:
            w1b_ref[...] = w1_ref[...].astype(bf16)
            w3b_ref[...] = w3_ref[...].astype(bf16)
            w2b_ref[...] = w2_ref[...].astype(bf16)

    def chunk(x):
        h1 = jnp.dot(x, w1b_ref[...], preferred_element_type=f32)
        h3 = jnp.dot(x, w3b_ref[...], preferred_element_type=f32)
        hid = (h1 * jax.nn.sigmoid(h1) * h3).astype(bf16)
        return jnp.dot(hid, w2b_ref[...], preferred_element_type=f32)

    @pl.when(c == 0)
    def _init():
        if grouped:
            o_ref[...] = jnp.zeros_like(o_ref)

            @pl.when(nsub > 0)
            def _():
                xb_ref[...] = x_ref[...].astype(bf16)
        else:
            o_ref[...] = res_ref[...]

    for k in range(1 if grouped else MLP_NSUB, MLP_NSUB + 1):
        @pl.when(nsub == k)
        def _():
            m = k * MLP_SUB
            cast_weights()
            o_ref[0:m, :] += chunk(xsrc[0:m, :])


def _mlp(x, w1, w3, w2, tile_expert, tile_nsub, tile_blk, *, res=None):
    grouped = res is None
    k = x.shape[1]
    f, n = w2.shape[1], w2.shape[2]
    n_tiles = tile_expert.shape[0]
    precast = w1.dtype == bf16
    fc = MLP_FC_BF16 if precast and f % MLP_FC_BF16 == 0 else MLP_FC
    n_chunks = f // fc

    def wcol(g, c, te, ns, tb):
        return jnp.where(ns[g] > 0, c, n_chunks - 1)

    in_specs = [
        pl.BlockSpec((MLP_ROWS, k), lambda g, c, te, ns, tb: (tb[g], 0)),
        pl.BlockSpec((None, k, fc), lambda g, c, te, ns, tb: (te[g], 0, wcol(g, c, te, ns, tb))),
        pl.BlockSpec((None, k, fc), lambda g, c, te, ns, tb: (te[g], 0, wcol(g, c, te, ns, tb))),
        pl.BlockSpec((None, fc, n), lambda g, c, te, ns, tb: (te[g], wcol(g, c, te, ns, tb), 0)),
    ]
    args = [x, w1, w3, w2]
    scratch = [] if precast else [pltpu.VMEM((k, fc), bf16), pltpu.VMEM((k, fc), bf16), pltpu.VMEM((fc, n), bf16)]
    if grouped:
        scratch = [pltpu.VMEM((MLP_ROWS, k), bf16)] + scratch
    else:
        in_specs.append(pl.BlockSpec((MLP_ROWS, n), lambda g, c, te, ns, tb: (tb[g], 0)))
        args.append(res)
    return pl.pallas_call(
        functools.partial(_mlp_kernel, grouped=grouped),
        grid_spec=pltpu.PrefetchScalarGridSpec(
            num_scalar_prefetch=3, grid=(n_tiles, n_chunks), in_specs=in_specs,
            out_specs=pl.BlockSpec((MLP_ROWS, n), lambda g, c, te, ns, tb: (g, 0)),
            scratch_shapes=scratch),
        out_shape=jax.ShapeDtypeStruct((x.shape[0], n), f32),
        compiler_params=_params("arbitrary", "arbitrary"), name="mlp_grouped" if grouped else "mlp_dense",
    )(tile_expert, tile_nsub, tile_blk, *args)


def _route_layout(idx, n_tiles):
    t = idx.shape[0]
    flat_e = idx[:, :TOP_K].reshape(-1)
    onehot = (flat_e[:, None] == jnp.arange(N_EXPERTS, dtype=i32)[None, :]).astype(i32)
    csum = jnp.cumsum(onehot, axis=0)
    rank = jnp.sum(onehot * csum, axis=1) - 1
    cnt = csum[-1]
    tiles_e = (cnt + MLP_ROWS - 1) // MLP_ROWS
    tile_end = jnp.cumsum(tiles_e)
    tile_start = tile_end - tiles_e
    pos = jnp.sum(onehot * tile_start[None, :], axis=1) * MLP_ROWS + rank
    n_used = tile_end[-1]
    g = jnp.arange(n_tiles, dtype=i32)
    g_eff = jnp.minimum(g, n_used - 1)
    tile_e = jnp.minimum(jnp.sum((tile_end[None, :] <= g_eff[:, None]).astype(i32), axis=1), N_EXPERTS - 1)
    tile_oh = (tile_e[:, None] == jnp.arange(N_EXPERTS, dtype=i32)[None, :]).astype(i32)
    rows_in = jnp.clip(jnp.sum(tile_oh * cnt[None, :], axis=1)
                       - (g_eff - jnp.sum(tile_oh * tile_start[None, :], axis=1)) * MLP_ROWS, 0, MLP_ROWS)
    nsub = jnp.where(g < n_used, (rows_in + MLP_SUB - 1) // MLP_SUB, 0).astype(i32)
    return pos.reshape(t, TOP_K), tile_start * MLP_ROWS, cnt, tile_e, nsub, g_eff.astype(i32)


def _row_copy(src_hbm, dst_ref, src_row, dst_row, sem):
    return pltpu.make_async_copy(src_hbm.at[pl.ds(src_row, 1)], dst_ref.at[pl.ds(dst_row, 1)], sem)


def _scatter_kernel(start_ref, cnt_ref, ns_ref, pos_ref, x_ref, o_hbm, zeros_ref, sem, zsem):
    i = pl.program_id(0)
    last = pl.num_programs(0) - 1
    n_tiles = o_hbm.shape[0] // MLP_ROWS

    def zero_fill(act):
        for g in range(n_tiles):
            for k in range(MLP_NSUB):
                @pl.when(ns_ref[g] <= k)
                def _():
                    r0 = g * MLP_ROWS + k * MLP_SUB
                    act(pltpu.make_async_copy(zeros_ref, o_hbm.at[pl.ds(r0, MLP_SUB)], zsem))
        for e in range(N_EXPERTS):
            tail = (-cnt_ref[e]) & (MLP_SUB - 1)
            first = start_ref[e] + cnt_ref[e]

            def one_row(j, carry):
                act(_row_copy(zeros_ref, o_hbm, 0, first + j, zsem))
                return carry

            lax.fori_loop(0, tail, one_row, 0)

    @pl.when(i == 0)
    def _():
        zeros_ref[...] = jnp.zeros_like(zeros_ref)
        zero_fill(lambda cp: cp.start())

    def issue(q, carry):
        r0 = pl.multiple_of(q * DMA_UNROLL, DMA_UNROLL)
        for u in range(DMA_UNROLL):
            for k in range(TOP_K):
                _row_copy(x_ref, o_hbm, r0 + u, pos_ref[0, 0, TOP_K * (r0 + u) + k], sem.at[k]).start(priority=k)
        return carry

    def drain(q, carry):
        for u in range(DMA_UNROLL):
            for k in range(TOP_K):
                _row_copy(x_ref, o_hbm, 0, 0, sem.at[k]).wait()
        return carry

    lax.fori_loop(0, SCATTER_ROWS // DMA_UNROLL, issue, 0)
    lax.fori_loop(0, SCATTER_ROWS // DMA_UNROLL, drain, 0)

    @pl.when(i == last)
    def _():
        zero_fill(lambda cp: cp.wait())


def _scatter_rows(x, pos, row_start, cnt, tile_nsub):
    t, d = x.shape
    n_tiles = tile_nsub.shape[0]
    steps = t // SCATTER_ROWS
    return pl.pallas_call(
        _scatter_kernel,
        grid_spec=pltpu.PrefetchScalarGridSpec(
            num_scalar_prefetch=3, grid=(steps,),
            in_specs=[pl.BlockSpec((1, 1, TOP_K * SCATTER_ROWS), lambda i, *_: (i, 0, 0), memory_space=pltpu.SMEM),
                      pl.BlockSpec((SCATTER_ROWS, d), lambda i, *_: (i, 0))],
            out_specs=pl.BlockSpec(memory_space=pl.ANY),
            scratch_shapes=[pltpu.VMEM((MLP_SUB, d), x.dtype), pltpu.SemaphoreType.DMA((TOP_K,)),
                            pltpu.SemaphoreType.DMA(())]),
        out_shape=jax.ShapeDtypeStruct((n_tiles * MLP_ROWS, d), x.dtype),
        compiler_params=_params("arbitrary"), name="moe_scatter",
    )(row_start, cnt, tile_nsub, pos.reshape(steps, 1, TOP_K * SCATTER_ROWS), x)


def _combine_kernel(pos_ref, pos_next_ref, h_ref, gates_ref, y_hbm, *rest, split_steps):
    if split_steps is None:
        o_ref, ya_ref, yb_ref, sem = rest
    else:
        gfin_ref, op_ref, os_ref, ya_ref, yb_ref, sem = rest
    i = pl.program_id(0)
    slot = i % 2

    def fetch(p_ref, s):
        def body(q, carry):
            for u in range(DMA_UNROLL):
                r = q * DMA_UNROLL + u
                _row_copy(y_hbm, ya_ref.at[s], p_ref[0, 0, TOP_K * r], r, sem.at[0, s]).start()
                _row_copy(y_hbm, yb_ref.at[s], p_ref[0, 0, TOP_K * r + 1], r, sem.at[1, s]).start(priority=1)
            return carry
        lax.fori_loop(0, COMBINE_ROWS // DMA_UNROLL, body, 0)

    @pl.when(i == 0)
    def _():
        fetch(pos_ref, 0)

    @pl.when(i + 1 < pl.num_programs(0))
    def _():
        fetch(pos_next_ref, 1 - slot)

    def drain(q, carry):
        for u in range(DMA_UNROLL):
            _row_copy(y_hbm, ya_ref.at[slot], 0, 0, sem.at[0, slot]).wait()
            _row_copy(y_hbm, yb_ref.at[slot], 0, 0, sem.at[1, slot]).wait()
        return carry

    lax.fori_loop(0, COMBINE_ROWS // DMA_UNROLL, drain, 0)
    gates = gates_ref[...]
    x = h_ref[...] + (gates[:, 0:1] * ya_ref[slot] + gates[:, 1:2] * yb_ref[slot])
    if split_steps is None:
        o_ref[...] = x
    else:
        y = _rms(x, gfin_ref[...])

        @pl.when(i < split_steps)
        def _():
            op_ref[...] = y

        @pl.when(i >= split_steps)
        def _():
            os_ref[...] = y


def _combine(h, gates, pos, y, *, final_norm=None, split_rows=None):
    t, d = h.shape
    rows = COMBINE_ROWS
    steps = t // rows
    pos3 = pos.reshape(steps, 1, TOP_K * rows)
    in_specs = [pl.BlockSpec((1, 1, TOP_K * rows), lambda i: (i, 0, 0), memory_space=pltpu.SMEM),
                pl.BlockSpec((1, 1, TOP_K * rows), lambda i: (jnp.minimum(i + 1, steps - 1), 0, 0),
                             memory_space=pltpu.SMEM),
                pl.BlockSpec((rows, d), lambda i: (i, 0)),
                pl.BlockSpec((rows, LANES), lambda i: (i, 0)),
                pl.BlockSpec(memory_space=pl.ANY)]
    args = [pos3, pos3, h, gates, y]
    if final_norm is None:
        split_steps = None
        out_specs = pl.BlockSpec((rows, d), lambda i: (i, 0))
        out_shape = jax.ShapeDtypeStruct((t, d), f32)
    else:
        split_steps = split_rows // rows
        in_specs.append(pl.BlockSpec((1, d), lambda i: (0, 0)))
        args.append(final_norm.reshape(1, d))
        out_specs = [pl.BlockSpec((rows, d), lambda i: (jnp.minimum(i, split_steps - 1), 0)),
                     pl.BlockSpec((rows, d), lambda i: (jnp.maximum(i - split_steps, 0), 0))]
        out_shape = [jax.ShapeDtypeStruct((split_rows, d), f32), jax.ShapeDtypeStruct((t - split_rows, d), f32)]
    return pl.pallas_call(
        functools.partial(_combine_kernel, split_steps=split_steps),
        grid=(steps,), in_specs=in_specs, out_specs=out_specs, out_shape=out_shape,
        scratch_shapes=[pltpu.VMEM((2, rows, d), f32), pltpu.VMEM((2, rows, d), f32),
                        pltpu.SemaphoreType.DMA((2, 2))],
        compiler_params=_params("arbitrary"), name="moe_combine" if final_norm is None else "moe_combine_final",
    )(*args)


def kernel(x_prompt, x_sample, state_conv, state_pool, norm_mix, norm_ffn, w_in, conv_w, pool_w, pool_scale,
           w_out, dense_w1, dense_w3, dense_w2, router_w, moe_w1, moe_w3, moe_w2, final_norm):
    n_p, len_p, d = x_prompt.shape
    n_s, len_s, _ = x_sample.shape
    t_p, t_s = n_p * len_p, n_s * len_s
    t = t_p + t_s
    depth = w_in.shape[0]
    n_moe, n_exp = moe_w1.shape[0], moe_w1.shape[1]
    x = (x_prompt.reshape(t_p, d), x_sample.reshape(t_s, d))
    router_pad = jnp.pad(router_w, ((0, 0), (0, 0), (0, LANES - router_w.shape[2])))
    moe_w1f = moe_w1.reshape((n_moe * n_exp,) + moe_w1.shape[2:])
    moe_w3f = moe_w3.reshape((n_moe * n_exp,) + moe_w3.shape[2:])
    moe_w2f = moe_w2.reshape((n_moe * n_exp,) + moe_w2.shape[2:])
    w_in_b, w_out_b = _to_bf16(w_in), _to_bf16(w_out)
    dense_b = [_to_bf16(w) for w in (dense_w1, dense_w3, dense_w2)]

    dense_tiles = t // MLP_ROWS
    dense_blk = jnp.arange(dense_tiles, dtype=i32)
    dense_nsub = jnp.full((dense_tiles,), MLP_NSUB, i32)
    moe_tiles = (TOP_K * t) // MLP_ROWS + n_exp

    conv_p, pool_p, conv_s, pool_s = [], [], [], []
    for l in range(depth):
        i = l // 2
        is_moe = l % 2 == 1
        outs = _block(x, l, norm_mix, w_in_b, w_out_b, state_conv, state_pool, conv_w, pool_w, pool_scale, norm_ffn,
                      router_pad if is_moe else None, i,
                      n_prompt_seq=n_p, prompt_len=len_p, n_sample_seq=n_s, sample_len=len_s)
        conv_p.append(outs[-4]); pool_p.append(outs[-3]); conv_s.append(outs[-2]); pool_s.append(outs[-1])
        if is_moe:
            h, hn, gates, idx = outs[:4]
            pos, row_start, cnt, tile_e, tile_nsub, tile_blk = _route_layout(idx, moe_tiles)
            xs = _scatter_rows(hn, pos, row_start, cnt, tile_nsub)
            ys = _mlp(xs, moe_w1f, moe_w3f, moe_w2f, tile_e + i * n_exp, tile_nsub, tile_blk)
            if l == depth - 1:
                y_prompt, y_sample = _combine(h, gates, pos, ys, final_norm=final_norm, split_rows=t_p)
            else:
                x = _combine(h, gates, pos, ys)
        else:
            h, hn = outs[:2]
            x = _mlp(hn, *dense_b, jnp.full((dense_tiles,), i, i32), dense_nsub, dense_blk, res=h)
            if l == depth - 1:
                y_prompt = _rmsnorm(x, final_norm, row_offset=0, rows=t_p)
                y_sample = _rmsnorm(x, final_norm, row_offset=t_p, rows=t_s)

    return (y_prompt.reshape(n_p, len_p, d), y_sample.reshape(n_s, len_s, d),
            jnp.stack(conv_p), jnp.stack(pool_p), jnp.stack(conv_s), jnp.stack(pool_s))
```

```python
import functools

import jax
import jax.numpy as jnp
from jax import lax
from jax.experimental import pallas as pl
from jax.experimental.pallas import tpu as pltpu

f32 = jnp.float32
bf16 = jnp.bfloat16
i32 = jnp.int32

CONV_DIM = 1024
POOL_DIM = 1024
IN_DIM = 3 * CONV_DIM + POOL_DIM
CONV_W = 3
POOL_WINDOWS = (2, 4, 8, 16)
POOL_GROUP_DIM = POOL_DIM // len(POOL_WINDOWS)
POOL_BUF = max(POOL_WINDOWS) - 1
N_EXPERTS = 8
TOP_K = 2
PAST_LEN = 16384
EPS = 1e-6

LANES = 128
HIST = 16
CONV_HIST = 8
MIX_ROWS = 256
SAMPLE_SEQS = 8
CAST_ROWS = 512
MLP_ROWS = 1024
MLP_SUB = 256
MLP_NSUB = MLP_ROWS // MLP_SUB
MLP_FC = 256
SCATTER_ROWS = 256
COMBINE_ROWS = 256
DMA_UNROLL = 256
VMEM_LIMIT = 60000 * 1024


def _params(*sem):
    return pltpu.CompilerParams(dimension_semantics=sem, vmem_limit_bytes=VMEM_LIMIT)


def _rms(x, g):
    return x * lax.rsqrt(jnp.mean(x * x, axis=-1, keepdims=True) + EPS) * g


def _norm_kernel(x_ref, g_ref, o_ref):
    o_ref[...] = _rms(x_ref[...], g_ref[...])


def _rmsnorm(x, g, *, row_offset, rows, tm=512):
    d = x.shape[1]
    off = row_offset // tm
    return pl.pallas_call(
        _norm_kernel, grid=(rows // tm,),
        in_specs=[pl.BlockSpec((tm, d), lambda i: (i + off, 0)), pl.BlockSpec((1, d), lambda i: (0, 0))],
        out_specs=pl.BlockSpec((tm, d), lambda i: (i, 0)),
        out_shape=jax.ShapeDtypeStruct((rows, d), f32),
        compiler_params=_params("arbitrary"), name="rmsnorm",
    )(x, g.reshape(1, d))


def _cast_kernel(w_ref, o_ref):
    o_ref[...] = w_ref[...].astype(o_ref.dtype)


def _to_bf16(w):
    l, k, n = w.shape
    return pl.pallas_call(
        _cast_kernel, grid=(l, k // CAST_ROWS),
        in_specs=[pl.BlockSpec((None, CAST_ROWS, n), lambda a, b: (a, b, 0))],
        out_specs=pl.BlockSpec((None, CAST_ROWS, n), lambda a, b: (a, b, 0)),
        out_shape=jax.ShapeDtypeStruct(w.shape, bf16),
        compiler_params=_params("arbitrary", "arbitrary"), name="cast_bf16",
    )(w)


def _block_kernel(*refs, with_router, split_x, prompt_steps, steps_per_seq, sample_len):
    if split_x:
        xp_ref, xs_ref = refs[:2]
        refs = refs[2:]
    else:
        xp_ref = xs_ref = refs[0]
        refs = refs[1:]
    gm_ref, win_ref, wout_ref, hv_ref, hu_ref, cw_ref, pw_ref, ps_ref, gf_ref = refs[:9]
    rest = refs[9:]
    if with_router:
        rw_ref, h_ref, hn_ref, gates_ref, idx_ref = rest[:5]
        rest = rest[5:]
    else:
        h_ref, hn_ref = rest[:2]
        rest = rest[2:]
    ncp_ref, npp_ref, ncs_ref, nps_ref, proj_ref, ycat_ref, ev_ref, eu_ref, ev3_ref, eu3_ref = rest
    i = pl.program_id(0)
    rows = MIX_ROWS
    n_groups = len(POOL_WINDOWS)

    c1, c2, c3 = CONV_DIM, 2 * CONV_DIM, 3 * CONV_DIM

    def conv_taps(ext):
        conv = cw_ref[0:1, :] * pltpu.roll(ext, 2, axis=0)
        conv = conv + cw_ref[1:2, :] * pltpu.roll(ext, 1, axis=0)
        return conv + cw_ref[2:3, :] * ext

    def window_sum(ext, w):
        s, k = ext, 1
        while k < w:
            s = s + pltpu.roll(s, k, axis=0)
            k *= 2
        return s

    def full_window_mean(g, s):
        w = POOL_WINDOWS[g]
        assert w & (w - 1) == 0, "scaling by 1/w equals dividing by w only for a power of two"
        return s * (1.0 / w)

    def pool_group_out(r0, n, g, mean, u):
        lo, hi = g * POOL_GROUP_DIM, (g + 1) * POOL_GROUP_DIM
        z = jnp.dot((mean - u).astype(bf16), pw_ref[g].astype(bf16), preferred_element_type=f32)
        ycat_ref[r0:r0 + n, c1 + lo:c1 + hi] = (z * ps_ref[:, lo:hi]).astype(bf16)

    def epilogue(x, mixed):
        h = x + mixed
        h_ref[...] = h
        hn = _rms(h, gf_ref[...])
        hn_ref[...] = hn.astype(hn_ref.dtype)
        if with_router:
            rw = rw_ref[...]
            rw_hi = rw.astype(bf16)
            rw_lo = (rw - rw_hi.astype(f32)).astype(bf16)
            hn_hi = hn.astype(bf16)
            hn_lo = (hn - hn_hi.astype(f32)).astype(bf16)
            half = rows // 2
            logits = jnp.concatenate([
                jnp.dot(hn_hi[r0:r0 + half], rw_hi, preferred_element_type=f32)
                + (jnp.dot(hn_lo[r0:r0 + half], rw_hi, preferred_element_type=f32)
                   + jnp.dot(hn_hi[r0:r0 + half], rw_lo, preferred_element_type=f32))
                for r0 in (0, half)], axis=0)
            lane = lax.broadcasted_iota(i32, logits.shape, 1)
            neg = jnp.float32(-jnp.inf)
            logits = jnp.where(lane < N_EXPERTS, logits, neg)
            m1 = jnp.max(logits, axis=-1, keepdims=True)
            i1 = jnp.min(jnp.where(logits == m1, lane, LANES), axis=-1, keepdims=True)
            rest_l = jnp.where(lane == i1, neg, logits)
            m2 = jnp.max(rest_l, axis=-1, keepdims=True)
            i2 = jnp.min(jnp.where(rest_l == m2, lane, LANES), axis=-1, keepdims=True)
            e = jnp.exp(m2 - m1)
            denom = 1.0 + e
            gates_ref[...] = jnp.where(lane == 0, 1.0 / denom, jnp.where(lane == 1, e / denom, 0.0))
            idx_ref[...] = jnp.where(lane == 0, i1, jnp.where(lane == 1, i2, 0))

    @pl.when(i < prompt_steps)
    def _prompt():
        t = i % steps_per_seq

        @pl.when(t == 0)
        def _():
            ev_ref[0:HIST, :] = jnp.zeros((HIST, CONV_DIM), f32)
            eu_ref[0:HIST, :] = jnp.zeros((HIST, POOL_DIM), f32)

        @pl.when(t > 0)
        def _():
            ev_ref[0:HIST, :] = ev_ref[rows:rows + HIST, :]
            eu_ref[0:HIST, :] = eu_ref[rows:rows + HIST, :]

        x = xp_ref[...]
        xn = _rms(x, gm_ref[...]).astype(bf16)
        ch = jnp.dot(xn, win_ref[:, c1:c3], preferred_element_type=f32)
        ev_ref[HIST:, :] = ch[:, 0:CONV_DIM] * ch[:, CONV_DIM:]
        b_gate = jnp.dot(xn, win_ref[:, 0:c1], preferred_element_type=f32)
        eu_ref[HIST:, :] = jnp.dot(xn, win_ref[:, c3:], preferred_element_type=f32)
        ycat_ref[:, 0:c1] = (b_gate * conv_taps(ev_ref[...])[HIST:, :]).astype(bf16)
        mixed = jnp.dot(ycat_ref[:, 0:c1], wout_ref[0:c1, :], preferred_element_type=f32)
        head = max(POOL_WINDOWS)
        assert rows >= head
        pos1 = lax.broadcasted_iota(i32, (head, 1), 0) + 1
        for g, w in enumerate(POOL_WINDOWS):
            ext = eu_ref[:, g * POOL_GROUP_DIM:(g + 1) * POOL_GROUP_DIM]
            s = window_sum(ext, w)[HIST:, :]
            filling = s[0:head] / jnp.minimum(pos1, w).astype(f32)
            mean = jnp.concatenate([jnp.where(t == 0, filling, full_window_mean(g, s[0:head])),
                                    full_window_mean(g, s[head:])], axis=0)
            pool_group_out(0, rows, g, mean, ext[HIST:, :])
        mixed = mixed + jnp.dot(ycat_ref[:, c1:], wout_ref[c1:, :], preferred_element_type=f32)
        epilogue(x, mixed)

        @pl.when(t == steps_per_seq - 1)
        def _():
            ncp_ref[0] = ev_ref[HIST + rows - (CONV_W - 1):HIST + rows, :]
            npp_ref[0] = eu_ref[HIST + rows - POOL_BUF:HIST + rows, :]

    @pl.when(i >= prompt_steps)
    def _sample():
        s, new = SAMPLE_SEQS, sample_len
        n = s * new
        nv, nu = s * (CONV_HIST + new), s * (HIST + new)
        assert PAST_LEN + 1 >= max(POOL_WINDOWS)
        assert new >= CONV_W - 1
        x = xs_ref[...]
        xn = _rms(x, gm_ref[...]).astype(bf16)
        proj_ref[...] = jnp.dot(xn, win_ref[...], preferred_element_type=f32)
        for part in range(rows // n):
            r0, s0 = part * n, part * s
            v3 = (proj_ref[r0:r0 + n, c1:c2] * proj_ref[r0:r0 + n, c2:c3]).reshape(s, new, CONV_DIM)
            ev3_ref[:, 0:CONV_HIST - (CONV_W - 1), :] = jnp.zeros((s, CONV_HIST - (CONV_W - 1), CONV_DIM), f32)
            ev3_ref[:, CONV_HIST - (CONV_W - 1):CONV_HIST, :] = hv_ref[s0:s0 + s]
            ev3_ref[:, CONV_HIST:, :] = v3
            eu3_ref[:, 0:HIST - POOL_BUF, :] = jnp.zeros((s, HIST - POOL_BUF, POOL_DIM), f32)
            eu3_ref[:, HIST - POOL_BUF:HIST, :] = hu_ref[s0:s0 + s]
            eu3_ref[:, HIST:, :] = proj_ref[r0:r0 + n, 3 * CONV_DIM:].reshape(s, new, POOL_DIM)
            conv = conv_taps(ev3_ref[...].reshape(nv, CONV_DIM))
            conv = conv.reshape(s, CONV_HIST + new, CONV_DIM)[:, CONV_HIST:, :].reshape(n, CONV_DIM)
            ycat_ref[r0:r0 + n, 0:c1] = (proj_ref[r0:r0 + n, 0:c1] * conv).astype(bf16)
            for g, w in enumerate(POOL_WINDOWS):
                lo, hi = g * POOL_GROUP_DIM, (g + 1) * POOL_GROUP_DIM
                sg = window_sum(eu3_ref[:, :, lo:hi].reshape(nu, POOL_GROUP_DIM), w)
                sg = sg.reshape(s, HIST + new, POOL_GROUP_DIM)[:, HIST:, :].reshape(n, POOL_GROUP_DIM)
                pool_group_out(r0, n, g, full_window_mean(g, sg), proj_ref[r0:r0 + n, c3 + lo:c3 + hi])
            ncs_ref[s0:s0 + s] = v3[:, new - (CONV_W - 1):, :]
            nps_ref[s0:s0 + s] = eu3_ref[:, HIST + new - POOL_BUF:, :]
        epilogue(x, jnp.dot(ycat_ref[...], wout_ref[...], preferred_element_type=f32))


def _block(x, layer, norm_mix, w_in_b, w_out_b, hv, hu, conv_w, pool_w, pool_scale, norm_ffn, router_w, moe_layer,
           *, n_prompt_seq, prompt_len, n_sample_seq, sample_len):
    split_x = isinstance(x, tuple)
    d = (x[0] if split_x else x).shape[1]
    t_all = sum(a.shape[0] for a in x) if split_x else x.shape[0]
    rows = MIX_ROWS
    steps_per_seq = prompt_len // rows
    prompt_steps = n_prompt_seq * steps_per_seq
    seqs_per_step = rows // sample_len
    sample_steps = n_sample_seq // seqs_per_step
    last_p = n_prompt_seq - 1
    n_groups = len(POOL_WINDOWS)
    with_router = router_w is not None
    once = pl.Buffered(1)

    def samp(i):
        return jnp.maximum(i - prompt_steps, 0)

    def pseq(i):
        return jnp.minimum(i // steps_per_seq, last_p)

    if split_x:
        x_specs = [pl.BlockSpec((rows, d), lambda i: (jnp.minimum(i, prompt_steps - 1), 0)),
                   pl.BlockSpec((rows, d), lambda i: (samp(i), 0))]
        x_args = list(x)
    else:
        x_specs = [pl.BlockSpec((rows, d), lambda i: (i, 0))]
        x_args = [x]
    in_specs = x_specs + [
                pl.BlockSpec((None, 1, d), lambda i: (layer, 0, 0)),
                pl.BlockSpec((None, d, IN_DIM), lambda i: (layer, 0, 0), pipeline_mode=once),
                pl.BlockSpec((None, CONV_DIM + POOL_DIM, d), lambda i: (layer, 0, 0), pipeline_mode=once),
                pl.BlockSpec((None, seqs_per_step, CONV_W - 1, CONV_DIM), lambda i: (layer, samp(i), 0, 0),
                             pipeline_mode=once),
                pl.BlockSpec((None, seqs_per_step, POOL_BUF, POOL_DIM), lambda i: (layer, samp(i), 0, 0),
                             pipeline_mode=once),
                pl.BlockSpec((None, CONV_W, CONV_DIM), lambda i: (layer, 0, 0)),
                pl.BlockSpec((None, n_groups, POOL_GROUP_DIM, POOL_GROUP_DIM), lambda i: (layer, 0, 0, 0)),
                pl.BlockSpec((None, 1, POOL_DIM), lambda i: (layer, 0, 0)),
                pl.BlockSpec((None, 1, d), lambda i: (layer, 0, 0))]
    args = x_args + [norm_mix.reshape(-1, 1, d), w_in_b, w_out_b, hv, hu, conv_w, pool_w,
                     pool_scale.reshape(-1, 1, POOL_DIM), norm_ffn.reshape(-1, 1, d)]
    out_specs = [pl.BlockSpec((rows, d), lambda i: (i, 0)), pl.BlockSpec((rows, d), lambda i: (i, 0))]
    out_shape = [jax.ShapeDtypeStruct((t_all, d), f32),
                 jax.ShapeDtypeStruct((t_all, d), f32 if with_router else bf16)]
    if with_router:
        in_specs.append(pl.BlockSpec((None, d, LANES), lambda i: (moe_layer, 0, 0)))
        args.append(router_w)
        out_specs += [pl.BlockSpec((rows, LANES), lambda i: (i, 0)), pl.BlockSpec((rows, LANES), lambda i: (i, 0))]
        out_shape += [jax.ShapeDtypeStruct((t_all, LANES), f32), jax.ShapeDtypeStruct((t_all, LANES), i32)]
    out_specs += [pl.BlockSpec((1, CONV_W - 1, CONV_DIM), lambda i: (pseq(i), 0, 0)),
                  pl.BlockSpec((1, POOL_BUF, POOL_DIM), lambda i: (pseq(i), 0, 0)),
                  pl.BlockSpec((seqs_per_step, CONV_W - 1, CONV_DIM), lambda i: (samp(i), 0, 0), pipeline_mode=once),
                  pl.BlockSpec((seqs_per_step, POOL_BUF, POOL_DIM), lambda i: (samp(i), 0, 0), pipeline_mode=once)]
    out_shape += [jax.ShapeDtypeStruct((n_prompt_seq, CONV_W - 1, CONV_DIM), f32),
                  jax.ShapeDtypeStruct((n_prompt_seq, POOL_BUF, POOL_DIM), f32),
                  jax.ShapeDtypeStruct((n_sample_seq, CONV_W - 1, CONV_DIM), f32),
                  jax.ShapeDtypeStruct((n_sample_seq, POOL_BUF, POOL_DIM), f32)]
    kern = functools.partial(_block_kernel, with_router=with_router, split_x=split_x, prompt_steps=prompt_steps,
                             steps_per_seq=steps_per_seq, sample_len=sample_len)
    return pl.pallas_call(
        kern, grid=(prompt_steps + sample_steps,), in_specs=in_specs, out_specs=out_specs, out_shape=out_shape,
        scratch_shapes=[pltpu.VMEM((rows, IN_DIM), f32), pltpu.VMEM((rows, CONV_DIM + POOL_DIM), bf16),
                        pltpu.VMEM((HIST + rows, CONV_DIM), f32), pltpu.VMEM((HIST + rows, POOL_DIM), f32),
                        pltpu.VMEM((SAMPLE_SEQS, CONV_HIST + sample_len, CONV_DIM), f32),
                        pltpu.VMEM((SAMPLE_SEQS, HIST + sample_len, POOL_DIM), f32)],
        compiler_params=_params("arbitrary"), name="block_router" if with_router else "block",
    )(*args)


def _mlp_kernel(te_ref, ns_ref, tb_ref, x_ref, w1_ref, w3_ref, w2_ref, *rest, grouped):
    del te_ref, tb_ref
    if grouped:
        o_ref, xb_ref, w1b_ref, w3b_ref, w2b_ref = rest
    else:
        res_ref, o_ref, w1b_ref, w3b_ref, w2b_ref = rest
    g, c = pl.program_id(0), pl.program_id(1)
    nsub = ns_ref[g]
    xsrc = xb_ref if grouped else x_ref

    def cast_weights():
        w1b_ref[...] = w1_ref[...].astype(bf16)
        w3b_ref[...] = w3_ref[...].astype(bf16)
        w2b_ref[...] = w2_ref[...].astype(bf16)

    def chunk(x):
        h1 = jnp.dot(x, w1b_ref[...], preferred_element_type=f32)
        h3 = jnp.dot(x, w3b_ref[...], preferred_element_type=f32)
        hid = (h1 * jax.nn.sigmoid(h1) * h3).astype(bf16)
        return jnp.dot(hid, w2b_ref[...], preferred_element_type=f32)

    @pl.when(c == 0)
    def _init():
        if grouped:
            o_ref[...] = jnp.zeros_like(o_ref)

            @pl.when(nsub > 0)
            def _():
                xb_ref[...] = x_ref[...].astype(bf16)
        else:
            o_ref[...] = res_ref[...]

    for k in range(1 if grouped else MLP_NSUB, MLP_NSUB + 1):
        @pl.when(nsub == k)
        def _():
            m = k * MLP_SUB
            cast_weights()
            o_ref[0:m, :] += chunk(xsrc[0:m, :])


def _mlp(x, w1, w3, w2, tile_expert, tile_nsub, tile_blk, *, res=None):
    grouped = res is None
    k = x.shape[1]
    f, n = w2.shape[1], w2.shape[2]
    n_tiles = tile_expert.shape[0]
    fc = MLP_FC
    n_chunks = f // fc

    def wcol(g, c, te, ns, tb):
        return jnp.where(ns[g] > 0, c, n_chunks - 1)

    in_specs = [
        pl.BlockSpec((MLP_ROWS, k), lambda g, c, te, ns, tb: (tb[g], 0)),
        pl.BlockSpec((None, k, fc), lambda g, c, te, ns, tb: (te[g], 0, wcol(g, c, te, ns, tb))),
        pl.BlockSpec((None, k, fc), lambda g, c, te, ns, tb: (te[g], 0, wcol(g, c, te, ns, tb))),
        pl.BlockSpec((None, fc, n), lambda g, c, te, ns, tb: (te[g], wcol(g, c, te, ns, tb), 0)),
    ]
    args = [x, w1, w3, w2]
    scratch = [pltpu.VMEM((k, fc), bf16), pltpu.VMEM((k, fc), bf16), pltpu.VMEM((fc, n), bf16)]
    if grouped:
        scratch = [pltpu.VMEM((MLP_ROWS, k), bf16)] + scratch
    else:
        in_specs.append(pl.BlockSpec((MLP_ROWS, n), lambda g, c, te, ns, tb: (tb[g], 0)))
        args.append(res)
    return pl.pallas_call(
        functools.partial(_mlp_kernel, grouped=grouped),
        grid_spec=pltpu.PrefetchScalarGridSpec(
            num_scalar_prefetch=3, grid=(n_tiles, n_chunks), in_specs=in_specs,
            out_specs=pl.BlockSpec((MLP_ROWS, n), lambda g, c, te, ns, tb: (g, 0)),
            scratch_shapes=scratch),
        out_shape=jax.ShapeDtypeStruct((x.shape[0], n), f32),
        compiler_params=_params("arbitrary", "arbitrary"), name="mlp_grouped" if grouped else "mlp_dense",
    )(tile_expert, tile_nsub, tile_blk, *args)


def _route_layout(idx, n_tiles):
    t = idx.shape[0]
    flat_e = idx[:, :TOP_K].reshape(-1)
    onehot = (flat_e[:, None] == jnp.arange(N_EXPERTS, dtype=i32)[None, :]).astype(i32)
    csum = jnp.cumsum(onehot, axis=0)
    rank = jnp.sum(onehot * csum, axis=1) - 1
    cnt = csum[-1]
    tiles_e = (cnt + MLP_ROWS - 1) // MLP_ROWS
    tile_end = jnp.cumsum(tiles_e)
    tile_start = tile_end - tiles_e
    pos = jnp.sum(onehot * tile_start[None, :], axis=1) * MLP_ROWS + rank
    n_used = tile_end[-1]
    g = jnp.arange(n_tiles, dtype=i32)
    g_eff = jnp.minimum(g, n_used - 1)
    tile_e = jnp.minimum(jnp.sum((tile_end[None, :] <= g_eff[:, None]).astype(i32), axis=1), N_EXPERTS - 1)
    tile_oh = (tile_e[:, None] == jnp.arange(N_EXPERTS, dtype=i32)[None, :]).astype(i32)
    rows_in = jnp.clip(jnp.sum(tile_oh * cnt[None, :], axis=1)
                       - (g_eff - jnp.sum(tile_oh * tile_start[None, :], axis=1)) * MLP_ROWS, 0, MLP_ROWS)
    nsub = jnp.where(g < n_used, (rows_in + MLP_SUB - 1) // MLP_SUB, 0).astype(i32)
    return pos.reshape(t, TOP_K), tile_start * MLP_ROWS, cnt, tile_e, nsub, g_eff.astype(i32)


def _row_copy(src_hbm, dst_ref, src_row, dst_row, sem):
    return pltpu.make_async_copy(src_hbm.at[pl.ds(src_row, 1)], dst_ref.at[pl.ds(dst_row, 1)], sem)


def _scatter_kernel(start_ref, cnt_ref, ns_ref, pos_ref, x_ref, o_hbm, zeros_ref, sem, zsem):
    i = pl.program_id(0)
    last = pl.num_programs(0) - 1
    n_tiles = o_hbm.shape[0] // MLP_ROWS

    def zero_fill(act):
        for g in range(n_tiles):
            for k in range(MLP_NSUB):
                @pl.when(ns_ref[g] <= k)
                def _():
                    r0 = g * MLP_ROWS + k * MLP_SUB
                    act(pltpu.make_async_copy(zeros_ref, o_hbm.at[pl.ds(r0, MLP_SUB)], zsem))
        for e in range(N_EXPERTS):
            tail = (-cnt_ref[e]) & (MLP_SUB - 1)
            first = start_ref[e] + cnt_ref[e]

            def one_row(j, carry):
                act(_row_copy(zeros_ref, o_hbm, 0, first + j, zsem))
                return carry

            lax.fori_loop(0, tail, one_row, 0)

    @pl.when(i == 0)
    def _():
        zeros_ref[...] = jnp.zeros_like(zeros_ref)
        zero_fill(lambda cp: cp.start())

    def issue(q, carry):
        r0 = pl.multiple_of(q * DMA_UNROLL, DMA_UNROLL)
        for u in range(DMA_UNROLL):
            for k in range(TOP_K):
                _row_copy(x_ref, o_hbm, r0 + u, pos_ref[0, 0, TOP_K * (r0 + u) + k], sem.at[k]).start(priority=k)
        return carry

    def drain(q, carry):
        for u in range(DMA_UNROLL):
            for k in range(TOP_K):
                _row_copy(x_ref, o_hbm, 0, 0, sem.at[k]).wait()
        return carry

    lax.fori_loop(0, SCATTER_ROWS // DMA_UNROLL, issue, 0)
    lax.fori_loop(0, SCATTER_ROWS // DMA_UNROLL, drain, 0)

    @pl.when(i == last)
    def _():
        zero_fill(lambda cp: cp.wait())


def _scatter_rows(x, pos, row_start, cnt, tile_nsub):
    t, d = x.shape
    n_tiles = tile_nsub.shape[0]
    steps = t // SCATTER_ROWS
    return pl.pallas_call(
        _scatter_kernel,
        grid_spec=pltpu.PrefetchScalarGridSpec(
            num_scalar_prefetch=3, grid=(steps,),
            in_specs=[pl.BlockSpec((1, 1, TOP_K * SCATTER_ROWS), lambda i, *_: (i, 0, 0), memory_space=pltpu.SMEM),
                      pl.BlockSpec((SCATTER_ROWS, d), lambda i, *_: (i, 0))],
            out_specs=pl.BlockSpec(memory_space=pl.ANY),
            scratch_shapes=[pltpu.VMEM((MLP_SUB, d), x.dtype), pltpu.SemaphoreType.DMA((TOP_K,)),
                            pltpu.SemaphoreType.DMA(())]),
        out_shape=jax.ShapeDtypeStruct((n_tiles * MLP_ROWS, d), x.dtype),
        compiler_params=_params("arbitrary"), name="moe_scatter",
    )(row_start, cnt, tile_nsub, pos.reshape(steps, 1, TOP_K * SCATTER_ROWS), x)


def _combine_kernel(pos_ref, pos_next_ref, h_ref, gates_ref, y_hbm, *rest, split_steps):
    if split_steps is None:
        o_ref, ya_ref, yb_ref, sem = rest
    else:
        gfin_ref, op_ref, os_ref, ya_ref, yb_ref, sem = rest
    i = pl.program_id(0)
    slot = i % 2

    def fetch(p_ref, s):
        def body(q, carry):
            for u in range(DMA_UNROLL):
                r = q * DMA_UNROLL + u
                _row_copy(y_hbm, ya_ref.at[s], p_ref[0, 0, TOP_K * r], r, sem.at[0, s]).start()
                _row_copy(y_hbm, yb_ref.at[s], p_ref[0, 0, TOP_K * r + 1], r, sem.at[1, s]).start(priority=1)
            return carry
        lax.fori_loop(0, COMBINE_ROWS // DMA_UNROLL, body, 0)

    @pl.when(i == 0)
    def _():
        fetch(pos_ref, 0)

    @pl.when(i + 1 < pl.num_programs(0))
    def _():
        fetch(pos_next_ref, 1 - slot)

    def drain(q, carry):
        for u in range(DMA_UNROLL):
            _row_copy(y_hbm, ya_ref.at[slot], 0, 0, sem.at[0, slot]).wait()
            _row_copy(y_hbm, yb_ref.at[slot], 0, 0, sem.at[1, slot]).wait()
        return carry

    lax.fori_loop(0, COMBINE_ROWS // DMA_UNROLL, drain, 0)
    gates = gates_ref[...]
    x = h_ref[...] + (gates[:, 0:1] * ya_ref[slot] + gates[:, 1:2] * yb_ref[slot])
    if split_steps is None:
        o_ref[...] = x
    else:
        y = _rms(x, gfin_ref[...])

        @pl.when(i < split_steps)
        def _():
            op_ref[...] = y

        @pl.when(i >= split_steps)
        def _():
            os_ref[...] = y


def _combine(h, gates, pos, y, *, final_norm=None, split_rows=None):
    t, d = h.shape
    rows = COMBINE_ROWS
    steps = t // rows
    pos3 = pos.reshape(steps, 1, TOP_K * rows)
    in_specs = [pl.BlockSpec((1, 1, TOP_K * rows), lambda i: (i, 0, 0), memory_space=pltpu.SMEM),
                pl.BlockSpec((1, 1, TOP_K * rows), lambda i: (jnp.minimum(i + 1, steps - 1), 0, 0),
                             memory_space=pltpu.SMEM),
                pl.BlockSpec((rows, d), lambda i: (i, 0)),
                pl.BlockSpec((rows, LANES), lambda i: (i, 0)),
                pl.BlockSpec(memory_space=pl.ANY)]
    args = [pos3, pos3, h, gates, y]
    if final_norm is None:
        split_steps = None
        out_specs = pl.BlockSpec((rows, d), lambda i: (i, 0))
        out_shape = jax.ShapeDtypeStruct((t, d), f32)
    else:
        split_steps = split_rows // rows
        in_specs.append(pl.BlockSpec((1, d), lambda i: (0, 0)))
        args.append(final_norm.reshape(1, d))
        out_specs = [pl.BlockSpec((rows, d), lambda i: (jnp.minimum(i, split_steps - 1), 0)),
                     pl.BlockSpec((rows, d), lambda i: (jnp.maximum(i - split_steps, 0), 0))]
        out_shape = [jax.ShapeDtypeStruct((split_rows, d), f32), jax.ShapeDtypeStruct((t - split_rows, d), f32)]
    return pl.pallas_call(
        functools.partial(_combine_kernel, split_steps=split_steps),
        grid=(steps,), in_specs=in_specs, out_specs=out_specs, out_shape=out_shape,
        scratch_shapes=[pltpu.VMEM((2, rows, d), f32), pltpu.VMEM((2, rows, d), f32),
                        pltpu.SemaphoreType.DMA((2, 2))],
        compiler_params=_params("arbitrary"), name="moe_combine" if final_norm is None else "moe_combine_final",
    )(*args)


def kernel(x_prompt, x_sample, state_conv, state_pool, norm_mix, norm_ffn, w_in, conv_w, pool_w, pool_scale,
           w_out, dense_w1, dense_w3, dense_w2, router_w, moe_w1, moe_w3, moe_w2, final_norm):
    n_p, len_p, d = x_prompt.shape
    n_s, len_s, _ = x_sample.shape
    t_p, t_s = n_p * len_p, n_s * len_s
    t = t_p + t_s
    depth = w_in.shape[0]
    n_moe, n_exp = moe_w1.shape[0], moe_w1.shape[1]
    x = (x_prompt.reshape(t_p, d), x_sample.reshape(t_s, d))
    router_pad = jnp.pad(router_w, ((0, 0), (0, 0), (0, LANES - router_w.shape[2])))
    moe_w1f = moe_w1.reshape((n_moe * n_exp,) + moe_w1.shape[2:])
    moe_w3f = moe_w3.reshape((n_moe * n_exp,) + moe_w3.shape[2:])
    moe_w2f = moe_w2.reshape((n_moe * n_exp,) + moe_w2.shape[2:])
    w_in_b, w_out_b = _to_bf16(w_in), _to_bf16(w_out)

    dense_tiles = t // MLP_ROWS
    dense_blk = jnp.arange(dense_tiles, dtype=i32)
    dense_nsub = jnp.full((dense_tiles,), MLP_NSUB, i32)
    moe_tiles = (TOP_K * t) // MLP_ROWS + n_exp

    conv_p, pool_p, conv_s, pool_s = [], [], [], []
    for l in range(depth):
        i = l // 2
        is_moe = l % 2 == 1
        outs = _block(x, l, norm_mix, w_in_b, w_out_b, state_conv, state_pool, conv_w, pool_w, pool_scale, norm_ffn,
                      router_pad if is_moe else None, i,
                      n_prompt_seq=n_p, prompt_len=len_p, n_sample_seq=n_s, sample_len=len_s)
        conv_p.append(outs[-4]); pool_p.append(outs[-3]); conv_s.append(outs[-2]); pool_s.append(outs[-1])
        if is_moe:
            h, hn, gates, idx = outs[:4]
            pos, row_start, cnt, tile_e, tile_nsub, tile_blk = _route_layout(idx, moe_tiles)
            xs = _scatter_rows(hn, pos, row_start, cnt, tile_nsub)
            ys = _mlp(xs, moe_w1f, moe_w3f, moe_w2f, tile_e + i * n_exp, tile_nsub, tile_blk)
            if l == depth - 1:
                y_prompt, y_sample = _combine(h, gates, pos, ys, final_norm=final_norm, split_rows=t_p)
            else:
                x = _combine(h, gates, pos, ys)
        else:
            h, hn = outs[:2]
            x = _mlp(hn, dense_w1, dense_w3, dense_w2, jnp.full((dense_tiles,), i, i32), dense_nsub, dense_blk, res=h)
            if l == depth - 1:
                y_prompt = _rmsnorm(x, final_norm, row_offset=0, rows=t_p)
                y_sample = _rmsnorm(x, final_norm, row_offset=t_p, rows=t_s)

    return (y_prompt.reshape(n_p, len_p, d), y_sample.reshape(n_s, len_s, d),
            jnp.stack(conv_p), jnp.stack(pool_p), jnp.stack(conv_s), jnp.stack(pool_s))
```
